```python
import math
import jax, jax.numpy as jnp
from jax import lax
import numpy as np

D_MODEL = 1024
BATCH = 2
SEQ = 16384
DEPTH = 2

CONV_DIM = 1024
CONV_KERNEL = 31
D_INNER = 2 * D_MODEL
HEAD_DIM = 64
N_SSM_HEADS = D_INNER // HEAD_DIM
N_GROUPS = 8
D_STATE = 128
SSM_CONV = 5
CHUNK = 128
XBC_DIM = D_INNER + 2 * N_GROUPS * D_STATE
FFN_DIM = int(math.ceil(D_MODEL * 8 / 3 / 256) * 256)
PLE_DIM = 256
N_IN = 2 * CONV_DIM + 2 * D_MODEL + D_INNER + XBC_DIM + 2 * N_SSM_HEADS
DEEPNORM_ALPHA = (2 * DEPTH) ** 0.25
DEEPNORM_BETA = (8 * DEPTH) ** -0.25
LN_EPS = 1e-5
RMS_EPS = 1e-6

kernel_name = "hybrid_conformer_ssd_encoder"


def layer_norm(x, g, b):
    xf = x.astype(jnp.float32)
    mu = jnp.mean(xf, axis=-1, keepdims=True)
    var = jnp.mean(jnp.square(xf - mu), axis=-1, keepdims=True)
    return ((xf - mu) * lax.rsqrt(var + LN_EPS) * g + b).astype(x.dtype)


def rms_norm(x, g):
    xf = x.astype(jnp.float32)
    return (xf * lax.rsqrt(jnp.mean(jnp.square(xf), axis=-1, keepdims=True) + RMS_EPS) * g).astype(x.dtype)


def depthwise_conv_centred(u, w, b):
    pad = (w.shape[0] - 1) // 2
    out = lax.conv_general_dilated(
        u, w[:, None, :].astype(u.dtype), window_strides=(1,), padding=[(pad, pad)],
        dimension_numbers=("NWC", "WIO", "NWC"), feature_group_count=u.shape[-1])
    return out + b


def ssd_chunked(x, dt, A, Bm, Cm):
    b, s, h, p = x.shape
    g, n = Bm.shape[-2:]
    r = h // g
    c, l = s // CHUNK, CHUNK
    x = x.astype(jnp.float32)
    dt = dt.astype(jnp.float32)
    X = (x * dt[..., None]).reshape(b, c, l, g, r, p)
    Ad = (dt * A).reshape(b, c, l, g, r).transpose(0, 1, 3, 4, 2)
    Bc = Bm.astype(jnp.float32).reshape(b, c, l, g, n)
    Cc = Cm.astype(jnp.float32).reshape(b, c, l, g, n)
    A_cs = jnp.cumsum(Ad, axis=-1)
    idx = jnp.arange(l)
    lower = idx[:, None] >= idx[None, :]
    seg = A_cs[..., :, None] - A_cs[..., None, :]
    Lmat = jnp.exp(jnp.where(lower, seg, -jnp.inf))
    CB = jnp.einsum("bclgn,bcsgn->bcgls", Cc, Bc)
    y_diag = jnp.einsum("bcgrls,bcsgrp->bclgrp", CB[:, :, :, None] * Lmat, X)
    decay_states = jnp.exp(A_cs[..., -1:] - A_cs).transpose(0, 1, 4, 2, 3)
    states = jnp.einsum("bclgn,bclgrp->bcgrpn", Bc, X * decay_states[..., None])
    chunk_decay = jnp.exp(A_cs[..., -1])

    def step(carry, inp):
        st, dec = inp
        return carry * dec[..., None, None] + st, carry

    h0 = jnp.zeros((b, g, r, p, n), jnp.float32)
    _, prev = lax.scan(step, h0, (jnp.moveaxis(states, 1, 0), jnp.moveaxis(chunk_decay, 1, 0)))
    prev = jnp.moveaxis(prev, 0, 1)
    decay_out = jnp.exp(A_cs).transpose(0, 1, 4, 2, 3)
    y_off = jnp.einsum("bclgn,bcgrpn->bclgrp", Cc, prev) * decay_out[..., None]
    return (y_diag + y_off).reshape(b, s, h, p)


def conformer_branch(glu_in, conv_w, conv_b, ln_g, ln_b, w_out):
    a, gt = jnp.split(glu_in, 2, axis=-1)
    u = a * jax.nn.sigmoid(gt)
    u = depthwise_conv_centred(u, conv_w, conv_b)
    u = jax.nn.silu(layer_norm(u, ln_g, ln_b))
    return u @ w_out


def mamba2_bidir_branch(z, xbc, dt_raw, conv_w, conv_b, a_log, dt_bias, d_skip, norm_g, w_out):
    b, s, _ = z.shape
    xbc = jax.nn.silu(depthwise_conv_centred(xbc, conv_w, conv_b))
    xs, Bm, Cm = jnp.split(xbc, [D_INNER, D_INNER + N_GROUPS * D_STATE], axis=-1)
    xs = xs.reshape(b, s, N_SSM_HEADS, HEAD_DIM)
    Bm = Bm.reshape(b, s, N_GROUPS, D_STATE)
    Cm = Cm.reshape(b, s, N_GROUPS, D_STATE)
    dt_raw = dt_raw.astype(jnp.float32)
    dt_f = jax.nn.softplus(dt_raw[..., :N_SSM_HEADS] + dt_bias[0])
    dt_b = jax.nn.softplus(dt_raw[..., N_SSM_HEADS:] + dt_bias[1])
    A = -jnp.exp(a_log.astype(jnp.float32))
    y_f = ssd_chunked(xs, dt_f, A[0], Bm, Cm)
    flip = lambda t: jnp.flip(t, axis=1)
    y_b = flip(ssd_chunked(flip(xs), flip(dt_b), A[1], flip(Bm), flip(Cm)))
    y = y_f + y_b + xs.astype(jnp.float32) * d_skip[:, None]
    y = y.reshape(b, s, D_INNER) * jax.nn.silu(z.astype(jnp.float32))
    yg = y.reshape(b, s, N_GROUPS, D_INNER // N_GROUPS)
    yg = yg * lax.rsqrt(jnp.mean(jnp.square(yg), axis=-1, keepdims=True) + RMS_EPS)
    y = (yg.reshape(b, s, D_INNER) * norm_g).astype(z.dtype)
    return y @ w_out


def setup_inputs(seed: int = 0) -> dict:
    key = jax.random.key(seed)
    ks = iter(jax.random.split(key, 32))

    def nrm(shape, scale):
        return jax.random.normal(next(ks), shape, jnp.float32) * scale

    L = DEPTH
    dt0 = jnp.exp(jax.random.uniform(next(ks), (L, 2, N_SSM_HEADS)) * (math.log(0.1) - math.log(1e-3)) + math.log(1e-3))
    dt_bias = dt0 + jnp.log(-jnp.expm1(-dt0))
    a_log = jnp.log(jax.random.uniform(next(ks), (L, 2, N_SSM_HEADS), minval=1.0, maxval=16.0))
    return {
        "x": nrm((BATCH, SEQ, D_MODEL), 1.0),
        "p": nrm((DEPTH, BATCH, SEQ, PLE_DIM), 1.0),
        "w_in": nrm((L, D_MODEL, N_IN), D_MODEL ** -0.5),
        "conv_a_w": nrm((L, CONV_KERNEL, CONV_DIM), CONV_KERNEL ** -0.5),
        "conv_a_b": nrm((L, CONV_DIM), 0.02),
        "ln_a_g": 1.0 + nrm((L, CONV_DIM), 0.02),
        "ln_a_b": nrm((L, CONV_DIM), 0.02),
        "w_a_out": nrm((L, CONV_DIM, D_MODEL), CONV_DIM ** -0.5 * DEEPNORM_BETA),
        "ssm_conv_w": nrm((L, SSM_CONV, XBC_DIM), SSM_CONV ** -0.5),
        "ssm_conv_b": nrm((L, XBC_DIM), 0.02),
        "a_log": a_log,
        "dt_bias": dt_bias,
        "d_skip": 1.0 + nrm((L, N_SSM_HEADS), 0.02),
        "ssm_norm_g": 1.0 + nrm((L, D_INNER), 0.02),
        "w_b_out": nrm((L, D_INNER, D_MODEL), D_INNER ** -0.5 * DEEPNORM_BETA),
        "w_o": nrm((L, D_MODEL, D_MODEL), D_MODEL ** -0.5 * DEEPNORM_BETA),
        "ln1_g": 1.0 + nrm((L, D_MODEL), 0.02),
        "ln1_b": nrm((L, D_MODEL), 0.02),
        "w_gate_up": nrm((L, D_MODEL, 2 * FFN_DIM), D_MODEL ** -0.5),
        "w_down": nrm((L, FFN_DIM, D_MODEL), FFN_DIM ** -0.5 * DEEPNORM_BETA),
        "ln2_g": 1.0 + nrm((L, D_MODEL), 0.02),
        "ln2_b": nrm((L, D_MODEL), 0.02),
        "w_ple": nrm((L, PLE_DIM, D_MODEL), PLE_DIM ** -0.5 * DEEPNORM_BETA),
        "ple_norm_g": 1.0 + nrm((L, D_MODEL), 0.02),
        "w_ple_gate": nrm((L, D_MODEL, D_MODEL), D_MODEL ** -0.5),
    }


def reference(x, p, w_in, conv_a_w, conv_a_b, ln_a_g, ln_a_b, w_a_out, ssm_conv_w, ssm_conv_b,
              a_log, dt_bias, d_skip, ssm_norm_g, w_b_out, w_o, ln1_g, ln1_b, w_gate_up, w_down,
              ln2_g, ln2_b, w_ple, ple_norm_g, w_ple_gate):
    cuts = [2 * CONV_DIM, 2 * CONV_DIM + 2 * D_MODEL, 2 * CONV_DIM + 2 * D_MODEL + D_INNER,
            2 * CONV_DIM + 2 * D_MODEL + D_INNER + XBC_DIM]
    for i in range(DEPTH):
        proj = x @ w_in[i]
        glu_in, gates, z, xbc, dt_raw = jnp.split(proj, cuts, axis=-1)
        gate_a, gate_b = jnp.split(gates, 2, axis=-1)
        y_a = conformer_branch(glu_in, conv_a_w[i], conv_a_b[i], ln_a_g[i], ln_a_b[i], w_a_out[i])
        y_b = mamba2_bidir_branch(z, xbc, dt_raw, ssm_conv_w[i], ssm_conv_b[i], a_log[i],
                                  dt_bias[i], d_skip[i], ssm_norm_g[i], w_b_out[i])
        merged = jax.nn.sigmoid(gate_a) * y_a + jax.nn.sigmoid(gate_b) * y_b
        h = layer_norm(DEEPNORM_ALPHA * x + merged @ w_o[i], ln1_g[i], ln1_b[i])
        g_, u_ = jnp.split(h @ w_gate_up[i], 2, axis=-1)
        h2 = layer_norm(DEEPNORM_ALPHA * h + (jax.nn.silu(g_) * u_) @ w_down[i], ln2_g[i], ln2_b[i])
        e = rms_norm(p[i] @ w_ple[i], ple_norm_g[i])
        x = h2 + e * jax.nn.sigmoid(h2 @ w_ple_gate[i])
    return x
```

```python
import functools
import math

import jax
import jax.numpy as jnp
from jax import lax
from jax.experimental import pallas as pl
from jax.experimental.pallas import tpu as pltpu

F32 = jnp.float32
BF16 = jnp.bfloat16

CONV_KERNEL = 31
SSM_CONV = 5
HEAD_DIM = 64
N_GROUPS = 8
D_STATE = 128
CHUNK = 128
LN_EPS = 1e-5
RMS_EPS = 1e-6

V7X_VMEM_LIMIT_BYTES = 56 * 1024 * 1024
SUBLANES = 8
LANES = 128
BF16_ROWS = 16

HI = lax.Precision.HIGHEST


def _sigmoid(v):
    return 1.0 / (1.0 + jnp.exp(-v))


def _softplus(v):
    return jnp.maximum(v, 0.0) + jnp.log1p(jnp.exp(-jnp.abs(v)))


def _layer_norm(v, g, b):
    mu = jnp.mean(v, axis=-1, keepdims=True)
    d = v - mu
    var = jnp.mean(d * d, axis=-1, keepdims=True)
    return d * lax.rsqrt(var + LN_EPS) * g + b


def _resident(shape):
    nd = len(shape)
    return pl.BlockSpec(shape, lambda *_: (0,) * nd, pipeline_mode=pl.Buffered(1))


def _params(*sem):
    return pltpu.CompilerParams(dimension_semantics=sem, vmem_limit_bytes=V7X_VMEM_LIMIT_BYTES)


def _inproj_kernel(x_ref, w_ref, u_ref, gates_ref, zs_ref, xbc_ref, dt_ref, *, nc, conv_dim, d_model, d_inner, xbc_dim):
    x = x_ref[...].astype(BF16)

    def mm(c0, n):
        return jnp.dot(x, w_ref[:, c0:c0 + n], preferred_element_type=F32)

    for k in range(conv_dim // nc):
        a = mm(k * nc, nc)
        g = mm(conv_dim + k * nc, nc)
        u_ref[:, k * nc:(k + 1) * nc] = (a * _sigmoid(g)).astype(BF16)
    off = 2 * conv_dim
    for k in range(2 * d_model // nc):
        gates_ref[:, k * nc:(k + 1) * nc] = _sigmoid(mm(off + k * nc, nc)).astype(BF16)
    off += 2 * d_model
    for k in range(d_inner // nc):
        z = mm(off + k * nc, nc)
        zs_ref[:, k * nc:(k + 1) * nc] = (z * _sigmoid(z)).astype(BF16)
    off += d_inner
    for k in range(xbc_dim // nc):
        xbc_ref[:, k * nc:(k + 1) * nc] = mm(off + k * nc, nc).astype(BF16)
    off += xbc_dim
    dt_ref[...] = mm(off, LANES)


def _inproj(x2d, w, *, conv_dim, d_model, d_inner, xbc_dim, tm=512, nc=512):
    m, d = x2d.shape
    n_all = w.shape[1]
    kern = functools.partial(_inproj_kernel, nc=nc, conv_dim=conv_dim, d_model=d_model, d_inner=d_inner, xbc_dim=xbc_dim)
    row = lambda i: (i, 0)
    return pl.pallas_call(
        kern,
        grid=(m // tm,),
        in_specs=[pl.BlockSpec((tm, d), row), _resident((d, n_all))],
        out_specs=[
            pl.BlockSpec((tm, conv_dim), row),
            pl.BlockSpec((tm, 2 * d_model), row),
            pl.BlockSpec((tm, d_inner), row),
            pl.BlockSpec((tm, xbc_dim), row),
            pl.BlockSpec((tm, LANES), row),
        ],
        out_shape=[
            jax.ShapeDtypeStruct((m, conv_dim), BF16),
            jax.ShapeDtypeStruct((m, 2 * d_model), BF16),
            jax.ShapeDtypeStruct((m, d_inner), BF16),
            jax.ShapeDtypeStruct((m, xbc_dim), BF16),
            jax.ShapeDtypeStruct((m, LANES), F32),
        ],
        compiler_params=_params("arbitrary"),
        name="inproj",
    )(x2d, w)


def _fill_window(win_ref, prev_ref, main_ref, next_ref, t, nt, halo, rows):
    win_ref[0:halo, :] = jnp.where(t > 0, prev_ref[...].astype(F32), 0.0)
    win_ref[halo:halo + rows, :] = main_ref[...].astype(F32)
    win_ref[halo + rows:, :] = jnp.where(t < nt - 1, next_ref[...].astype(F32), 0.0)


def _dwconv_block(win_ref, w_ref, r0, cs, rb, halo, ktaps):
    pad = (ktaps - 1) // 2
    lo = (halo - pad) // SUBLANES * SUBLANES
    hi = -(-(halo + pad + rb) // SUBLANES) * SUBLANES
    n = hi - lo
    wv = win_ref[pl.ds(r0 + lo, n), cs]
    by_shift = {}
    for k in range(ktaps):
        q, r = divmod(halo - lo + k - pad, SUBLANES)
        by_shift.setdefault(r, []).append((k, q))
    acc = None
    for r, taps in by_shift.items():
        sh = wv if r == 0 else pltpu.roll(wv, n - r, axis=0)
        for k, q in taps:
            term = w_ref[k:k + 1, cs] * sh[SUBLANES * q:SUBLANES * q + rb]
            acc = term if acc is None else acc + term
    return acc


def _halo_specs(t_rows, halo, s_len, c):
    hb = t_rows // halo
    last = s_len // halo - 1
    return [
        pl.BlockSpec((None, halo, c), lambda b, t: (b, jnp.maximum(t * hb - 1, 0), 0)),
        pl.BlockSpec((None, t_rows, c), lambda b, t: (b, t, 0)),
        pl.BlockSpec((None, halo, c), lambda b, t: (b, jnp.minimum((t + 1) * hb, last), 0)),
    ]


def _conformer_kernel(prev_ref, main_ref, next_ref, ga_ref, cw_ref, cb_ref, lg_ref, lb_ref, wo_ref, o_ref,
                      win_ref, act_ref, *, t_rows, nt, rb, halo):
    t = pl.program_id(1)
    c = main_ref.shape[-1]
    _fill_window(win_ref, prev_ref, main_ref, next_ref, t, nt, halo, t_rows)

    def body(i, carry):
        r0 = pl.multiple_of(i * rb, rb)
        for j in range(c // LANES):
            cs = slice(j * LANES, (j + 1) * LANES)
            acc = _dwconv_block(win_ref, cw_ref, r0, cs, rb, halo, CONV_KERNEL)
            act_ref[pl.ds(r0, rb), cs] = acc + cb_ref[:, cs]
        return carry

    lax.fori_loop(0, t_rows // rb, body, 0)
    v = _layer_norm(act_ref[...], lg_ref[...], lb_ref[...])
    s = (v * _sigmoid(v)).astype(BF16)
    y = jnp.dot(s, wo_ref[...], preferred_element_type=F32)
    o_ref[...] = (y * ga_ref[...].astype(F32)).astype(BF16)


def _conformer(u, gates, cw, cb, lg, lb, wo, *, t_rows=512, rb=32):
    b, s, c = u.shape
    d = wo.shape[1]
    nt = s // t_rows
    halo = BF16_ROWS
    kern = functools.partial(_conformer_kernel, t_rows=t_rows, nt=nt, rb=rb, halo=halo)
    return pl.pallas_call(
        kern,
        grid=(b, nt),
        in_specs=_halo_specs(t_rows, halo, s, c) + [
            pl.BlockSpec((None, t_rows, d), lambda bi, t: (bi, t, 0)),
            _resident(cw.shape), _resident(cb.shape), _resident(lg.shape), _resident(lb.shape), _resident(wo.shape),
        ],
        out_specs=pl.BlockSpec((None, t_rows, d), lambda bi, t: (bi, t, 0)),
        out_shape=jax.ShapeDtypeStruct((b, s, d), BF16),
        scratch_shapes=[pltpu.VMEM((t_rows + 2 * halo, c), F32), pltpu.VMEM((t_rows, c), F32)],
        compiler_params=_params("arbitrary", "arbitrary"),
        name="conformer",
    )(u, u, u, gates, cw, cb, lg, lb, wo)


def _expand_heads(v, e_ref):
    hi = v.astype(BF16)
    lo = (v - hi.astype(F32)).astype(BF16)
    e = e_ref[...]
    return jnp.dot(hi, e, preferred_element_type=F32) + jnp.dot(lo, e, preferred_element_type=F32)


def _tri_masks():
    li = lax.broadcasted_iota(jnp.int32, (CHUNK, CHUNK), 0)
    si = lax.broadcasted_iota(jnp.int32, (CHUNK, CHUNK), 1)
    return li, si


def _dt_and_a(dt_raw, dtb_ref, alog_ref):
    dt = _softplus(dt_raw + dtb_ref[...])
    a = dt * (-jnp.exp(alog_ref[...]))
    return dt, a


def _ssd_bwd_kernel(prev_ref, main_ref, next_ref, dt_ref, cw_ref, cb_ref, dtb_ref, alog_ref, eb_ref,
                    xc_ref, yoff_ref, win_ref, state_ref, *, t_rows, nt, rb, halo, d_inner):
    tt = pl.program_id(1)
    t = nt - 1 - tt
    c = main_ref.shape[-1]
    _fill_window(win_ref, prev_ref, main_ref, next_ref, t, nt, halo, t_rows)

    def conv_body(i, carry):
        r0 = pl.multiple_of(i * rb, rb)
        for j in range(c // LANES):
            cs = slice(j * LANES, (j + 1) * LANES)
            v = _dwconv_block(win_ref, cw_ref, r0, cs, rb, halo, SSM_CONV) + cb_ref[:, cs]
            xc_ref[pl.ds(r0, rb), cs] = (v * _sigmoid(v)).astype(BF16)
        return carry

    lax.fori_loop(0, t_rows // rb, conv_body, 0)

    @pl.when(tt == 0)
    def _():
        state_ref[...] = jnp.zeros_like(state_ref)

    li, si = _tri_masks()
    triu = (si >= li).astype(F32)
    gw = d_inner // N_GROUPS
    for ci in reversed(range(t_rows // CHUNK)):
        rows = slice(ci * CHUNK, (ci + 1) * CHUNK)
        dt, a = _dt_and_a(dt_ref[rows, :], dtb_ref, alog_ref)
        rcs = jnp.dot(triu, a, precision=HI, preferred_element_type=F32)
        tot = rcs[0:1, :]
        sb = _expand_heads(dt * jnp.exp(tot - rcs), eb_ref)
        ob = _expand_heads(jnp.exp(rcs), eb_ref)
        dec = _expand_heads(jnp.broadcast_to(jnp.exp(tot), (SUBLANES, LANES)), eb_ref)[0:1, :]
        for g in range(N_GROUPS):
            gs = slice(g * gw, (g + 1) * gw)
            x_g = xc_ref[rows, gs]
            b_g = xc_ref[rows, d_inner + g * D_STATE:d_inner + (g + 1) * D_STATE]
            c_g = xc_ref[rows, d_inner + (N_GROUPS + g) * D_STATE:d_inner + (N_GROUPS + g + 1) * D_STATE]
            st = state_ref[g]
            yoff_ref[rows, gs] = jnp.dot(c_g, st.astype(BF16), preferred_element_type=F32) * ob[:, gs]
            xdec = (x_g.astype(F32) * sb[:, gs]).astype(BF16)
            upd = lax.dot_general(b_g, xdec, (((0,), (0,)), ((), ())), preferred_element_type=F32)
            state_ref[g] = st * dec[:, gs] + upd


def _ssd_bwd(xbc, dt_raw, cw, cb, dtb, alog, eb, *, d_inner, t_rows=256, rb=32):
    b, s, c = xbc.shape
    nt = s // t_rows
    halo = BF16_ROWS
    hb = t_rows // halo
    last = s // halo - 1
    rev = lambda t: nt - 1 - t
    kern = functools.partial(_ssd_bwd_kernel, t_rows=t_rows, nt=nt, rb=rb, halo=halo, d_inner=d_inner)
    return pl.pallas_call(
        kern,
        grid=(b, nt),
        in_specs=[
            pl.BlockSpec((None, halo, c), lambda bi, t: (bi, jnp.maximum(rev(t) * hb - 1, 0), 0)),
            pl.BlockSpec((None, t_rows, c), lambda bi, t: (bi, rev(t), 0)),
            pl.BlockSpec((None, halo, c), lambda bi, t: (bi, jnp.minimum((rev(t) + 1) * hb, last), 0)),
            pl.BlockSpec((None, t_rows, LANES), lambda bi, t: (bi, rev(t), 0)),
            _resident(cw.shape), _resident(cb.shape), _resident(dtb.shape), _resident(alog.shape), _resident(eb.shape),
        ],
        out_specs=[
            pl.BlockSpec((None, t_rows, c), lambda bi, t: (bi, rev(t), 0)),
            pl.BlockSpec((None, t_rows, d_inner), lambda bi, t: (bi, rev(t), 0)),
        ],
        out_shape=[jax.ShapeDtypeStruct((b, s, c), BF16), jax.ShapeDtypeStruct((b, s, d_inner), F32)],
        scratch_shapes=[
            pltpu.VMEM((t_rows + 2 * halo, c), F32),
            pltpu.VMEM((N_GROUPS, D_STATE, d_inner // N_GROUPS), F32),
        ],
        compiler_params=_params("arbitrary", "arbitrary"),
        name="ssd_bwd",
    )(xbc, xbc, xbc, dt_raw, cw, cb, dtb, alog, eb)


def _ssd_fwd_kernel(xc_ref, dt_ref, yoffb_ref, zs_ref, ya_ref, gb_ref, dtb_ref, alog_ref, dtbc_ref, alogc_ref, ef_ref,
                    dsk_ref, ng_ref, wbo_ref, o_ref, state_ref, yn_ref, *, t_rows, d_inner):
    t = pl.program_id(1)

    @pl.when(t == 0)
    def _():
        state_ref[...] = jnp.zeros_like(state_ref)

    li, si = _tri_masks()
    tril = (si <= li).astype(F32)
    triu = (si >= li).astype(F32)
    lower = li > si
    upper = li < si
    gw = d_inner // N_GROUPS
    hpg = gw // HEAD_DIM
    nh = d_inner // HEAD_DIM
    for ci in range(t_rows // CHUNK):
        rows = slice(ci * CHUNK, (ci + 1) * CHUNK)
        dt_raw = dt_ref[rows, :]
        dt, a = _dt_and_a(dt_raw, dtb_ref, alog_ref)
        acs = jnp.dot(tril, a, precision=HI, preferred_element_type=F32)
        rcs = jnp.dot(triu, a, precision=HI, preferred_element_type=F32)
        totf = acs[CHUNK - 1:CHUNK, :]
        sf = _expand_heads(dt * jnp.exp(totf - acs), ef_ref)
        of = _expand_heads(jnp.exp(acs), ef_ref)
        dec = _expand_heads(jnp.broadcast_to(jnp.exp(totf), (SUBLANES, LANES)), ef_ref)[0:1, :]
        dtT = _softplus(dt_raw.T + dtbc_ref[...])
        aT = dtT * (-jnp.exp(alogc_ref[...]))
        acsT = jnp.dot(aT, triu, precision=HI, preferred_element_type=F32)
        rcsT = jnp.dot(aT, tril, precision=HI, preferred_element_type=F32)
        logdtT = jnp.log(dtT)
        rowf = acsT - logdtT
        rowb = rcsT - logdtT
        drow = jnp.log(dtT[0:nh, :] + dtT[nh:2 * nh, :])
        for g in range(N_GROUPS):
            gs = slice(g * gw, (g + 1) * gw)
            x_g = xc_ref[rows, gs]
            b_g = xc_ref[rows, d_inner + g * D_STATE:d_inner + (g + 1) * D_STATE]
            c_g = xc_ref[rows, d_inner + (N_GROUPS + g) * D_STATE:d_inner + (N_GROUPS + g + 1) * D_STATE]
            cbm = lax.dot_general(c_g, b_g, (((1,), (1,)), ((), ())), preferred_element_type=F32)
            ys = []
            for r in range(hpg):
                h = g * hpg + r
                argf = acs[:, h:h + 1] - rowf[h:h + 1, :]
                argb = rcs[:, nh + h:nh + h + 1] - rowb[nh + h:nh + h + 1, :]
                arg = jnp.where(lower, argf, jnp.where(upper, argb, drow[h:h + 1, :]))
                w = (cbm * jnp.exp(arg)).astype(BF16)
                ys.append(jnp.dot(w, x_g[:, r * HEAD_DIM:(r + 1) * HEAD_DIM], preferred_element_type=F32))
            y = jnp.concatenate(ys, axis=-1)
            st = state_ref[g]
            xf = x_g.astype(F32)
            y = y + jnp.dot(c_g, st.astype(BF16), preferred_element_type=F32) * of[:, gs]
            y = y + yoffb_ref[rows, gs] + xf * dsk_ref[:, gs]
            xdec = (xf * sf[:, gs]).astype(BF16)
            upd = lax.dot_general(b_g, xdec, (((0,), (0,)), ((), ())), preferred_element_type=F32)
            state_ref[g] = st * dec[:, gs] + upd
            yz = y * zs_ref[rows, gs].astype(F32)
            ms = jnp.mean(yz * yz, axis=-1, keepdims=True)
            yn_ref[rows, gs] = (yz * lax.rsqrt(ms + RMS_EPS) * ng_ref[:, gs]).astype(BF16)
    yb = jnp.dot(yn_ref[...], wbo_ref[...], preferred_element_type=F32)
    o_ref[...] = (ya_ref[...].astype(F32) + gb_ref[...].astype(F32) * yb).astype(BF16)


def _ssd_fwd(xc, dt_raw, yoffb, zs, ya, gates, dtb, alog, ef, dsk, ng, wbo, *, d_inner, t_rows=256):
    b, s, c = xc.shape
    dtbc, alogc = dtb.reshape(-1, 1), alog.reshape(-1, 1)
    d = wbo.shape[1]
    nt = s // t_rows
    kern = functools.partial(_ssd_fwd_kernel, t_rows=t_rows, d_inner=d_inner)
    tile = lambda w: pl.BlockSpec((None, t_rows, w), lambda bi, t: (bi, t, 0))
    return pl.pallas_call(
        kern,
        grid=(b, nt),
        in_specs=[
            tile(c), tile(LANES), tile(d_inner), tile(d_inner), tile(d),
            pl.BlockSpec((None, t_rows, d), lambda bi, t: (bi, t, 1)),
            _resident(dtb.shape), _resident(alog.shape), _resident(dtbc.shape), _resident(alogc.shape),
            _resident(ef.shape), _resident(dsk.shape), _resident(ng.shape), _resident(wbo.shape),
        ],
        out_specs=tile(d),
        out_shape=jax.ShapeDtypeStruct((b, s, d), BF16),
        scratch_shapes=[
            pltpu.VMEM((N_GROUPS, D_STATE, d_inner // N_GROUPS), F32),
            pltpu.VMEM((t_rows, d_inner), BF16),
        ],
        compiler_params=_params("arbitrary", "arbitrary"),
        name="ssd_fwd",
    )(xc, dt_raw, yoffb, zs, ya, gates, dtb, alog, dtbc, alogc, ef, dsk, ng, wbo)


def _ffn_kernel(x_ref, m_ref, p_ref, wo_ref, l1g_ref, l1b_ref, wgu_ref, wd_ref, l2g_ref, l2b_ref, wple_ref,
                pg_ref, wpg_ref, o_ref, *, alpha, ffn, fc):
    mix = jnp.dot(m_ref[...], wo_ref[...], preferred_element_type=F32)
    h = _layer_norm(alpha * x_ref[...] + mix, l1g_ref[...], l1b_ref[...])
    hb = h.astype(BF16)
    acc = None
    for k in range(ffn // fc):
        g = jnp.dot(hb, wgu_ref[:, k * fc:(k + 1) * fc], preferred_element_type=F32)
        u = jnp.dot(hb, wgu_ref[:, ffn + k * fc:ffn + (k + 1) * fc], preferred_element_type=F32)
        act = (g * _sigmoid(g) * u).astype(BF16)
        part = jnp.dot(act, wd_ref[k * fc:(k + 1) * fc, :], preferred_element_type=F32)
        acc = part if acc is None else acc + part
    h2 = _layer_norm(alpha * h + acc, l2g_ref[...], l2b_ref[...])
    e = jnp.dot(p_ref[...].astype(BF16), wple_ref[...], preferred_element_type=F32)
    e = e * lax.rsqrt(jnp.mean(e * e, axis=-1, keepdims=True) + RMS_EPS) * pg_ref[...]
    gate = _sigmoid(jnp.dot(h2.astype(BF16), wpg_ref[...], preferred_element_type=F32))
    o_ref[...] = h2 + e * gate


def _ffn(x2d, merged2d, p2d, wo, l1g, l1b, wgu, wd, l2g, l2b, wple, pg, wpg, *, alpha, tm=512):
    m, d = x2d.shape
    ffn = wd.shape[0]
    fc = ffn // 2
    assert fc % LANES == 0
    kern = functools.partial(_ffn_kernel, alpha=alpha, ffn=ffn, fc=fc)
    row = lambda i: (i, 0)
    return pl.pallas_call(
        kern,
        grid=(m // tm,),
        in_specs=[
            pl.BlockSpec((tm, d), row), pl.BlockSpec((tm, d), row), pl.BlockSpec((tm, p2d.shape[1]), row),
            _resident(wo.shape), _resident(l1g.shape), _resident(l1b.shape), _resident(wgu.shape), _resident(wd.shape),
            _resident(l2g.shape), _resident(l2b.shape), _resident(wple.shape), _resident(pg.shape), _resident(wpg.shape),
        ],
        out_specs=pl.BlockSpec((tm, d), row),
        out_shape=jax.ShapeDtypeStruct((m, d), F32),
        compiler_params=_params("arbitrary"),
        name="ffn",
    )(x2d, merged2d, p2d, wo, l1g, l1b, wgu, wd, l2g, l2b, wple, pg, wpg)


def _head_expand_matrix(first_slot, n_heads, d_inner):
    slot = lax.broadcasted_iota(jnp.int32, (LANES, d_inner), 0)
    head = lax.broadcasted_iota(jnp.int32, (LANES, d_inner), 1) // HEAD_DIM
    return (slot == head + first_slot).astype(BF16)


def kernel(x, p, w_in, conv_a_w, conv_a_b, ln_a_g, ln_a_b, w_a_out, ssm_conv_w, ssm_conv_b, a_log, dt_bias, d_skip, ssm_norm_g, w_b_out, w_o, ln1_g, ln1_b, w_gate_up, w_down, ln2_g, ln2_b, w_ple, ple_norm_g, w_ple_gate):
    b, s, d_model = x.shape
    depth = w_in.shape[0]
    conv_dim = conv_a_w.shape[-1]
    d_inner = w_b_out.shape[1]
    xbc_dim = ssm_conv_w.shape[-1]
    n_heads = d_inner // HEAD_DIM
    m = b * s
    alpha = (2 * depth) ** 0.25
    n_main = 2 * conv_dim + 2 * d_model + d_inner + xbc_dim
    assert xbc_dim == d_inner + 2 * N_GROUPS * D_STATE and 2 * n_heads <= LANES

    ef = _head_expand_matrix(0, n_heads, d_inner)
    eb = _head_expand_matrix(n_heads, n_heads, d_inner)
    row = lambda v: v.reshape(1, -1).astype(F32)
    slots = lambda v: jnp.pad(v.reshape(1, -1).astype(F32), ((0, 0), (0, LANES - 2 * n_heads)))

    x2d = x.reshape(m, d_model)
    for i in range(depth):
        w = jnp.pad(w_in[i], ((0, 0), (0, LANES - 2 * n_heads))).astype(BF16)
        assert w.shape[1] == n_main + LANES
        u, gates, zs, xbc, dt_raw = _inproj(x2d, w, conv_dim=conv_dim, d_model=d_model, d_inner=d_inner, xbc_dim=xbc_dim)
        ya = _conformer(u.reshape(b, s, conv_dim), gates.reshape(b, s, 2 * d_model), conv_a_w[i], row(conv_a_b[i]),
                        row(ln_a_g[i]), row(ln_a_b[i]), w_a_out[i].astype(BF16))
        dtb, alog = slots(dt_bias[i]), slots(a_log[i])
        dt3 = dt_raw.reshape(b, s, LANES)
        xc, yoffb = _ssd_bwd(xbc.reshape(b, s, xbc_dim), dt3, ssm_conv_w[i], row(ssm_conv_b[i]), dtb, alog, eb,
                             d_inner=d_inner)
        merged = _ssd_fwd(xc, dt3, yoffb, zs.reshape(b, s, d_inner), ya, gates.reshape(b, s, 2 * d_model), dtb, alog, ef,
                          row(jnp.repeat(d_skip[i], HEAD_DIM)), row(ssm_norm_g[i]), w_b_out[i].astype(BF16),
                          d_inner=d_inner)
        x2d = _ffn(x2d, merged.reshape(m, d_model), p[i].reshape(m, -1), w_o[i].astype(BF16), row(ln1_g[i]), row(ln1_b[i]),
                   w_gate_up[i].astype(BF16), w_down[i].astype(BF16), row(ln2_g[i]), row(ln2_b[i]),
                   w_ple[i].astype(BF16), row(ple_norm_g[i]), w_ple_gate[i].astype(BF16), alpha=alpha)
    return x2d.reshape(b, s, d_model)
```

```python
import functools
import math

import jax
import jax.numpy as jnp
from jax import lax
from jax.experimental import pallas as pl
from jax.experimental.pallas import tpu as pltpu

F32 = jnp.float32
BF16 = jnp.bfloat16

CONV_KERNEL = 31
SSM_CONV = 5
HEAD_DIM = 64
N_GROUPS = 8
D_STATE = 128
CHUNK = 128
LN_EPS = 1e-5
RMS_EPS = 1e-6

V7X_VMEM_LIMIT_BYTES = 56 * 1024 * 1024
SUBLANES = 8
LANES = 128
BF16_ROWS = 16

HI = lax.Precision.HIGHEST


def _sigmoid(v):
    return 1.0 / (1.0 + jnp.exp(-v))


def _softplus(v):
    return jnp.maximum(v, 0.0) + jnp.log1p(jnp.exp(-jnp.abs(v)))


def _layer_norm(v, g, b):
    mu = jnp.mean(v, axis=-1, keepdims=True)
    d = v - mu
    var = jnp.mean(d * d, axis=-1, keepdims=True)
    return d * lax.rsqrt(var + LN_EPS) * g + b


def _resident(shape):
    nd = len(shape)
    return pl.BlockSpec(shape, lambda *_: (0,) * nd, pipeline_mode=pl.Buffered(1))


def _params(*sem):
    return pltpu.CompilerParams(dimension_semantics=sem, vmem_limit_bytes=V7X_VMEM_LIMIT_BYTES)


def _inproj_kernel(x_ref, w_ref, u_ref, gates_ref, zs_ref, xbc_ref, dt_ref, *, nc, conv_dim, d_model, d_inner, xbc_dim):
    x = x_ref[...].astype(BF16)

    def mm(c0, n):
        return jnp.dot(x, w_ref[:, c0:c0 + n], preferred_element_type=F32)

    for k in range(conv_dim // nc):
        a = mm(k * nc, nc)
        g = mm(conv_dim + k * nc, nc)
        u_ref[:, k * nc:(k + 1) * nc] = (a * _sigmoid(g)).astype(BF16)
    off = 2 * conv_dim
    for k in range(2 * d_model // nc):
        gates_ref[:, k * nc:(k + 1) * nc] = _sigmoid(mm(off + k * nc, nc)).astype(BF16)
    off += 2 * d_model
    for k in range(d_inner // nc):
        z = mm(off + k * nc, nc)
        zs_ref[:, k * nc:(k + 1) * nc] = (z * _sigmoid(z)).astype(BF16)
    off += d_inner
    for k in range(xbc_dim // nc):
        xbc_ref[:, k * nc:(k + 1) * nc] = mm(off + k * nc, nc).astype(BF16)
    off += xbc_dim
    dt_ref[...] = mm(off, LANES)


def _inproj(x2d, w, *, conv_dim, d_model, d_inner, xbc_dim, tm=512, nc=512):
    m, d = x2d.shape
    n_all = w.shape[1]
    kern = functools.partial(_inproj_kernel, nc=nc, conv_dim=conv_dim, d_model=d_model, d_inner=d_inner, xbc_dim=xbc_dim)
    row = lambda i: (i, 0)
    return pl.pallas_call(
        kern,
        grid=(m // tm,),
        in_specs=[pl.BlockSpec((tm, d), row), _resident((d, n_all))],
        out_specs=[
            pl.BlockSpec((tm, conv_dim), row),
            pl.BlockSpec((tm, 2 * d_model), row),
            pl.BlockSpec((tm, d_inner), row),
            pl.BlockSpec((tm, xbc_dim), row),
            pl.BlockSpec((tm, LANES), row),
        ],
        out_shape=[
            jax.ShapeDtypeStruct((m, conv_dim), BF16),
            jax.ShapeDtypeStruct((m, 2 * d_model), BF16),
            jax.ShapeDtypeStruct((m, d_inner), BF16),
            jax.ShapeDtypeStruct((m, xbc_dim), BF16),
            jax.ShapeDtypeStruct((m, LANES), F32),
        ],
        compiler_params=_params("arbitrary"),
        name="inproj",
    )(x2d, w)


def _fill_window(win_ref, prev_ref, main_ref, next_ref, t, nt, halo, rows):
    win_ref[0:halo, :] = jnp.where(t > 0, prev_ref[...].astype(F32), 0.0)
    win_ref[halo:halo + rows, :] = main_ref[...].astype(F32)
    win_ref[halo + rows:, :] = jnp.where(t < nt - 1, next_ref[...].astype(F32), 0.0)


def _dwconv_block(win_ref, w_ref, r0, cs, rb, halo, ktaps):
    pad = (ktaps - 1) // 2
    lo = (halo - pad) // SUBLANES * SUBLANES
    hi = -(-(halo + pad + rb) // SUBLANES) * SUBLANES
    n = hi - lo
    wv = win_ref[pl.ds(r0 + lo, n), cs]
    by_shift = {}
    for k in range(ktaps):
        q, r = divmod(halo - lo + k - pad, SUBLANES)
        by_shift.setdefault(r, []).append((k, q))
    acc = None
    for r, taps in by_shift.items():
        sh = wv if r == 0 else pltpu.roll(wv, n - r, axis=0)
        for k, q in taps:
            term = w_ref[k:k + 1, cs] * sh[SUBLANES * q:SUBLANES * q + rb]
            acc = term if acc is None else acc + term
    return acc


def _halo_specs(t_rows, halo, s_len, c):
    hb = t_rows // halo
    last = s_len // halo - 1
    return [
        pl.BlockSpec((None, halo, c), lambda b, t: (b, jnp.maximum(t * hb - 1, 0), 0)),
        pl.BlockSpec((None, t_rows, c), lambda b, t: (b, t, 0)),
        pl.BlockSpec((None, halo, c), lambda b, t: (b, jnp.minimum((t + 1) * hb, last), 0)),
    ]


def _conformer_kernel(prev_ref, main_ref, next_ref, ga_ref, cw_ref, cb_ref, lg_ref, lb_ref, wo_ref, o_ref,
                      win_ref, act_ref, *, t_rows, nt, rb, halo):
    t = pl.program_id(1)
    c = main_ref.shape[-1]
    _fill_window(win_ref, prev_ref, main_ref, next_ref, t, nt, halo, t_rows)

    def body(i, carry):
        r0 = pl.multiple_of(i * rb, rb)
        for j in range(c // LANES):
            cs = slice(j * LANES, (j + 1) * LANES)
            acc = _dwconv_block(win_ref, cw_ref, r0, cs, rb, halo, CONV_KERNEL)
            act_ref[pl.ds(r0, rb), cs] = acc + cb_ref[:, cs]
        return carry

    lax.fori_loop(0, t_rows // rb, body, 0)
    v = _layer_norm(act_ref[...], lg_ref[...], lb_ref[...])
    s = (v * _sigmoid(v)).astype(BF16)
    y = jnp.dot(s, wo_ref[...], preferred_element_type=F32)
    o_ref[...] = (y * ga_ref[...].astype(F32)).astype(BF16)


def _conformer(u, gates, cw, cb, lg, lb, wo, *, t_rows=512, rb=32):
    b, s, c = u.shape
    d = wo.shape[1]
    nt = s // t_rows
    halo = BF16_ROWS
    kern = functools.partial(_conformer_kernel, t_rows=t_rows, nt=nt, rb=rb, halo=halo)
    return pl.pallas_call(
        kern,
        grid=(b, nt),
        in_specs=_halo_specs(t_rows, halo, s, c) + [
            pl.BlockSpec((None, t_rows, d), lambda bi, t: (bi, t, 0)),
            _resident(cw.shape), _resident(cb.shape), _resident(lg.shape), _resident(lb.shape), _resident(wo.shape),
        ],
        out_specs=pl.BlockSpec((None, t_rows, d), lambda bi, t: (bi, t, 0)),
        out_shape=jax.ShapeDtypeStruct((b, s, d), BF16),
        scratch_shapes=[pltpu.VMEM((t_rows + 2 * halo, c), F32), pltpu.VMEM((t_rows, c), F32)],
        compiler_params=_params("arbitrary", "arbitrary"),
        name="conformer",
    )(u, u, u, gates, cw, cb, lg, lb, wo)


def _expand_heads(v, e_ref):
    return jnp.dot(v.astype(BF16), e_ref[...], preferred_element_type=F32)


def _tri_masks():
    li = lax.broadcasted_iota(jnp.int32, (CHUNK, CHUNK), 0)
    si = lax.broadcasted_iota(jnp.int32, (CHUNK, CHUNK), 1)
    return li, si


def _dt_and_a(dt_raw, dtb_ref, alog_ref):
    dt = _softplus(dt_raw + dtb_ref[...])
    a = dt * (-jnp.exp(alog_ref[...]))
    return dt, a


def _ssd_bwd_kernel(prev_ref, main_ref, next_ref, dt_ref, cw_ref, cb_ref, dtb_ref, alog_ref, eb_ref,
                    xc_ref, yoff_ref, win_ref, state_ref, *, t_rows, nt, rb, halo, d_inner):
    tt = pl.program_id(1)
    t = nt - 1 - tt
    c = main_ref.shape[-1]
    _fill_window(win_ref, prev_ref, main_ref, next_ref, t, nt, halo, t_rows)

    def conv_body(i, carry):
        r0 = pl.multiple_of(i * rb, rb)
        for j in range(c // LANES):
            cs = slice(j * LANES, (j + 1) * LANES)
            v = _dwconv_block(win_ref, cw_ref, r0, cs, rb, halo, SSM_CONV) + cb_ref[:, cs]
            xc_ref[pl.ds(r0, rb), cs] = (v * _sigmoid(v)).astype(BF16)
        return carry

    lax.fori_loop(0, t_rows // rb, conv_body, 0)

    @pl.when(tt == 0)
    def _():
        state_ref[...] = jnp.zeros_like(state_ref)

    li, si = _tri_masks()
    triu = (si >= li).astype(F32)
    gw = d_inner // N_GROUPS
    for ci in reversed(range(t_rows // CHUNK)):
        rows = slice(ci * CHUNK, (ci + 1) * CHUNK)
        dt, a = _dt_and_a(dt_ref[rows, :], dtb_ref, alog_ref)
        rcs = jnp.dot(triu, a, precision=HI, preferred_element_type=F32)
        tot = rcs[0:1, :]
        sb = _expand_heads(dt * jnp.exp(tot - rcs), eb_ref)
        ob = _expand_heads(jnp.exp(rcs), eb_ref)
        dec = _expand_heads(jnp.broadcast_to(jnp.exp(tot), (SUBLANES, LANES)), eb_ref)[0:1, :]
        for g in range(N_GROUPS):
            gs = slice(g * gw, (g + 1) * gw)
            x_g = xc_ref[rows, gs]
            b_g = xc_ref[rows, d_inner + g * D_STATE:d_inner + (g + 1) * D_STATE]
            c_g = xc_ref[rows, d_inner + (N_GROUPS + g) * D_STATE:d_inner + (N_GROUPS + g + 1) * D_STATE]
            st = state_ref[g]
            yoff_ref[rows, gs] = jnp.dot(c_g, st.astype(BF16), preferred_element_type=F32) * ob[:, gs]
            xdec = (x_g.astype(F32) * sb[:, gs]).astype(BF16)
            upd = lax.dot_general(b_g, xdec, (((0,), (0,)), ((), ())), preferred_element_type=F32)
            state_ref[g] = st * dec[:, gs] + upd


def _ssd_bwd(xbc, dt_raw, cw, cb, dtb, alog, eb, *, d_inner, t_rows=256, rb=32):
    b, s, c = xbc.shape
    nt = s // t_rows
    halo = BF16_ROWS
    hb = t_rows // halo
    last = s // halo - 1
    rev = lambda t: nt - 1 - t
    kern = functools.partial(_ssd_bwd_kernel, t_rows=t_rows, nt=nt, rb=rb, halo=halo, d_inner=d_inner)
    return pl.pallas_call(
        kern,
        grid=(b, nt),
        in_specs=[
            pl.BlockSpec((None, halo, c), lambda bi, t: (bi, jnp.maximum(rev(t) * hb - 1, 0), 0)),
            pl.BlockSpec((None, t_rows, c), lambda bi, t: (bi, rev(t), 0)),
            pl.BlockSpec((None, halo, c), lambda bi, t: (bi, jnp.minimum((rev(t) + 1) * hb, last), 0)),
            pl.BlockSpec((None, t_rows, LANES), lambda bi, t: (bi, rev(t), 0)),
            _resident(cw.shape), _resident(cb.shape), _resident(dtb.shape), _resident(alog.shape), _resident(eb.shape),
        ],
        out_specs=[
            pl.BlockSpec((None, t_rows, c), lambda bi, t: (bi, rev(t), 0)),
            pl.BlockSpec((None, t_rows, d_inner), lambda bi, t: (bi, rev(t), 0)),
        ],
        out_shape=[jax.ShapeDtypeStruct((b, s, c), BF16), jax.ShapeDtypeStruct((b, s, d_inner), F32)],
        scratch_shapes=[
            pltpu.VMEM((t_rows + 2 * halo, c), F32),
            pltpu.VMEM((N_GROUPS, D_STATE, d_inner // N_GROUPS), F32),
        ],
        compiler_params=_params("arbitrary", "arbitrary"),
        name="ssd_bwd",
    )(xbc, xbc, xbc, dt_raw, cw, cb, dtb, alog, eb)


def _ssd_fwd_kernel(xc_ref, dt_ref, yoffb_ref, zs_ref, ya_ref, gb_ref, dtb_ref, alog_ref, ef_ref,
                    dsk_ref, ng_ref, wbo_ref, o_ref, state_ref, yn_ref, *, t_rows, d_inner):
    t = pl.program_id(1)

    @pl.when(t == 0)
    def _():
        state_ref[...] = jnp.zeros_like(state_ref)

    li, si = _tri_masks()
    tril = (si <= li).astype(F32)
    triu = (si >= li).astype(F32)
    lower = li > si
    upper = li < si
    gw = d_inner // N_GROUPS
    hpg = gw // HEAD_DIM
    nh = d_inner // HEAD_DIM
    for ci in range(t_rows // CHUNK):
        rows = slice(ci * CHUNK, (ci + 1) * CHUNK)
        dt_raw = dt_ref[rows, :]
        dt, a = _dt_and_a(dt_raw, dtb_ref, alog_ref)
        acs = jnp.dot(tril, a, precision=HI, preferred_element_type=F32)
        totf = acs[CHUNK - 1:CHUNK, :]
        rcs = totf - acs + a
        sf = _expand_heads(dt * jnp.exp(totf - acs), ef_ref)
        of = _expand_heads(jnp.exp(acs), ef_ref)
        dec = _expand_heads(jnp.broadcast_to(jnp.exp(totf), (SUBLANES, LANES)), ef_ref)[0:1, :]
        logdt = jnp.log(dt)
        rowf = (acs - logdt).T
        rowb = (rcs - logdt).T
        dtT = dt.T
        drow = jnp.log(dtT[0:nh, :] + dtT[nh:2 * nh, :])
        lane_lo = si < HEAD_DIM
        for g in range(N_GROUPS):
            gs = slice(g * gw, (g + 1) * gw)
            x_g = xc_ref[rows, gs]
            b_g = xc_ref[rows, d_inner + g * D_STATE:d_inner + (g + 1) * D_STATE]
            c_g = xc_ref[rows, d_inner + (N_GROUPS + g) * D_STATE:d_inner + (N_GROUPS + g + 1) * D_STATE]
            cbm = lax.dot_general(c_g, b_g, (((1,), (1,)), ((), ())), preferred_element_type=F32)
            ys = []
            for pr in range(hpg // 2):
                ws = []
                for h in (g * hpg + 2 * pr, g * hpg + 2 * pr + 1):
                    argf = acs[:, h:h + 1] - rowf[h:h + 1, :]
                    argb = rcs[:, nh + h:nh + h + 1] - rowb[nh + h:nh + h + 1, :]
                    arg = jnp.where(lower, argf, jnp.where(upper, argb, drow[h:h + 1, :]))
                    ws.append((cbm * jnp.exp(arg)).astype(BF16))
                xp = x_g[:, pr * LANES:(pr + 1) * LANES]
                zero = jnp.zeros_like(xp)
                xbd = jnp.concatenate([jnp.where(lane_lo, xp, zero), jnp.where(lane_lo, zero, xp)], axis=0)
                ys.append(jnp.dot(jnp.concatenate(ws, axis=1), xbd, preferred_element_type=F32))
            y = jnp.concatenate(ys, axis=-1)
            st = state_ref[g]
            xf = x_g.astype(F32)
            y = y + jnp.dot(c_g, st.astype(BF16), preferred_element_type=F32) * of[:, gs]
            y = y + yoffb_ref[rows, gs] + xf * dsk_ref[:, gs]
            xdec = (xf * sf[:, gs]).astype(BF16)
            upd = lax.dot_general(b_g, xdec, (((0,), (0,)), ((), ())), preferred_element_type=F32)
            state_ref[g] = st * dec[:, gs] + upd
            yz = y * zs_ref[rows, gs].astype(F32)
            ms = jnp.mean(yz * yz, axis=-1, keepdims=True)
            yn_ref[rows, gs] = (yz * lax.rsqrt(ms + RMS_EPS) * ng_ref[:, gs]).astype(BF16)
    yb = jnp.dot(yn_ref[...], wbo_ref[...], preferred_element_type=F32)
    o_ref[...] = (ya_ref[...].astype(F32) + gb_ref[...].astype(F32) * yb).astype(BF16)


def _ssd_fwd(xc, dt_raw, yoffb, zs, ya, gates, dtb, alog, ef, dsk, ng, wbo, *, d_inner, t_rows=256):
    b, s, c = xc.shape
    d = wbo.shape[1]
    nt = s // t_rows
    kern = functools.partial(_ssd_fwd_kernel, t_rows=t_rows, d_inner=d_inner)
    tile = lambda w: pl.BlockSpec((None, t_rows, w), lambda bi, t: (bi, t, 0))
    return pl.pallas_call(
        kern,
        grid=(b, nt),
        in_specs=[
            tile(c), tile(LANES), tile(d_inner), tile(d_inner), tile(d),
            pl.BlockSpec((None, t_rows, d), lambda bi, t: (bi, t, 1)),
            _resident(dtb.shape), _resident(alog.shape), _resident(ef.shape), _resident(dsk.shape), _resident(ng.shape), _resident(wbo.shape),
        ],
        out_specs=tile(d),
        out_shape=jax.ShapeDtypeStruct((b, s, d), BF16),
        scratch_shapes=[
            pltpu.VMEM((N_GROUPS, D_STATE, d_inner // N_GROUPS), F32),
            pltpu.VMEM((t_rows, d_inner), BF16),
        ],
        compiler_params=_params("arbitrary", "arbitrary"),
        name="ssd_fwd",
    )(xc, dt_raw, yoffb, zs, ya, gates, dtb, alog, ef, dsk, ng, wbo)


def _ffn_kernel(x_ref, m_ref, p_ref, wo_ref, l1g_ref, l1b_ref, wgu_ref, wd_ref, l2g_ref, l2b_ref, wple_ref,
                pg_ref, wpg_ref, o_ref, *, alpha, ffn, fc):
    mix = jnp.dot(m_ref[...], wo_ref[...], preferred_element_type=F32)
    h = _layer_norm(alpha * x_ref[...] + mix, l1g_ref[...], l1b_ref[...])
    hb = h.astype(BF16)
    acc = None
    for k in range(ffn // fc):
        g = jnp.dot(hb, wgu_ref[:, k * fc:(k + 1) * fc], preferred_element_type=F32)
        u = jnp.dot(hb, wgu_ref[:, ffn + k * fc:ffn + (k + 1) * fc], preferred_element_type=F32)
        act = (g * _sigmoid(g) * u).astype(BF16)
        part = jnp.dot(act, wd_ref[k * fc:(k + 1) * fc, :], preferred_element_type=F32)
        acc = part if acc is None else acc + part
    h2 = _layer_norm(alpha * h + acc, l2g_ref[...], l2b_ref[...])
    e = jnp.dot(p_ref[...].astype(BF16), wple_ref[...], preferred_element_type=F32)
    e = e * lax.rsqrt(jnp.mean(e * e, axis=-1, keepdims=True) + RMS_EPS) * pg_ref[...]
    gate = _sigmoid(jnp.dot(h2.astype(BF16), wpg_ref[...], preferred_element_type=F32))
    o_ref[...] = h2 + e * gate


def _ffn(x2d, merged2d, p2d, wo, l1g, l1b, wgu, wd, l2g, l2b, wple, pg, wpg, *, alpha, tm=512):
    m, d = x2d.shape
    ffn = wd.shape[0]
    fc = ffn // 2
    assert fc % LANES == 0
    kern = functools.partial(_ffn_kernel, alpha=alpha, ffn=ffn, fc=fc)
    row = lambda i: (i, 0)
    return pl.pallas_call(
        kern,
        grid=(m // tm,),
        in_specs=[
            pl.BlockSpec((tm, d), row), pl.BlockSpec((tm, d), row), pl.BlockSpec((tm, p2d.shape[1]), row),
            _resident(wo.shape), _resident(l1g.shape), _resident(l1b.shape), _resident(wgu.shape), _resident(wd.shape),
            _resident(l2g.shape), _resident(l2b.shape), _resident(wple.shape), _resident(pg.shape), _resident(wpg.shape),
        ],
        out_specs=pl.BlockSpec((tm, d), row),
        out_shape=jax.ShapeDtypeStruct((m, d), F32),
        compiler_params=_params("arbitrary"),
        name="ffn",
    )(x2d, merged2d, p2d, wo, l1g, l1b, wgu, wd, l2g, l2b, wple, pg, wpg)


def _head_expand_matrix(first_slot, n_heads, d_inner):
    slot = lax.broadcasted_iota(jnp.int32, (LANES, d_inner), 0)
    head = lax.broadcasted_iota(jnp.int32, (LANES, d_inner), 1) // HEAD_DIM
    return (slot == head + first_slot).astype(BF16)


def kernel(x, p, w_in, conv_a_w, conv_a_b, ln_a_g, ln_a_b, w_a_out, ssm_conv_w, ssm_conv_b, a_log, dt_bias, d_skip, ssm_norm_g, w_b_out, w_o, ln1_g, ln1_b, w_gate_up, w_down, ln2_g, ln2_b, w_ple, ple_norm_g, w_ple_gate):
    b, s, d_model = x.shape
    depth = w_in.shape[0]
    conv_dim = conv_a_w.shape[-1]
    d_inner = w_b_out.shape[1]
    xbc_dim = ssm_conv_w.shape[-1]
    n_heads = d_inner // HEAD_DIM
    m = b * s
    alpha = (2 * depth) ** 0.25
    n_main = 2 * conv_dim + 2 * d_model + d_inner + xbc_dim
    assert xbc_dim == d_inner + 2 * N_GROUPS * D_STATE and 2 * n_heads <= LANES and 2 * HEAD_DIM == LANES

    ef = _head_expand_matrix(0, n_heads, d_inner)
    eb = _head_expand_matrix(n_heads, n_heads, d_inner)
    row = lambda v: v.reshape(1, -1).astype(F32)
    slots = lambda v: jnp.pad(v.reshape(1, -1).astype(F32), ((0, 0), (0, LANES - 2 * n_heads)))

    x2d = x.reshape(m, d_model)
    for i in range(depth):
        w = jnp.pad(w_in[i], ((0, 0), (0, LANES - 2 * n_heads))).astype(BF16)
        assert w.shape[1] == n_main + LANES
        u, gates, zs, xbc, dt_raw = _inproj(x2d, w, conv_dim=conv_dim, d_model=d_model, d_inner=d_inner, xbc_dim=xbc_dim)
        ya = _conformer(u.reshape(b, s, conv_dim), gates.reshape(b, s, 2 * d_model), conv_a_w[i], row(conv_a_b[i]),
                        row(ln_a_g[i]), row(ln_a_b[i]), w_a_out[i].astype(BF16))
        dtb, alog = slots(dt_bias[i]), slots(a_log[i])
        dt3 = dt_raw.reshape(b, s, LANES)
        xc, yoffb = _ssd_bwd(xbc.reshape(b, s, xbc_dim), dt3, ssm_conv_w[i], row(ssm_conv_b[i]), dtb, alog, eb,
                             d_inner=d_inner)
        merged = _ssd_fwd(xc, dt3, yoffb, zs.reshape(b, s, d_inner), ya, gates.reshape(b, s, 2 * d_model), dtb, alog, ef,
                          row(jnp.repeat(d_skip[i], HEAD_DIM)), row(ssm_norm_g[i]), w_b_out[i].astype(BF16),
                          d_inner=d_inner)
        x2d = _ffn(x2d, merged.reshape(m, d_model), p[i].reshape(m, -1), w_o[i].astype(BF16), row(ln1_g[i]), row(ln1_b[i]),
                   w_gate_up[i].astype(BF16), w_down[i].astype(BF16), row(ln2_g[i]), row(ln2_b[i]),
                   w_ple[i].astype(BF16), row(ple_norm_g[i]), w_ple_gate[i].astype(BF16), alpha=alpha)
    return x2d.reshape(b, s, d_model)
```

```python
import functools

import jax
import jax.numpy as jnp
from jax import lax
from jax.experimental import pallas as pl
from jax.experimental.pallas import tpu as pltpu

F32 = jnp.float32
BF16 = jnp.bfloat16

CONV_KERNEL = 31
SSM_CONV = 5
HEAD_DIM = 64
N_GROUPS = 8
D_STATE = 128
CHUNK = 128
LN_EPS = 1e-5
RMS_EPS = 1e-6

V7X_VMEM_LIMIT_BYTES = 56 * 1024 * 1024
SUBLANES = 8
LANES = 128
BF16_ROWS = 16

HI = lax.Precision.HIGHEST


def _sigmoid(v):
    return 1.0 / (1.0 + jnp.exp(-v))


def _softplus(v):
    return jnp.maximum(v, 0.0) + jnp.log1p(jnp.exp(-jnp.abs(v)))


def _layer_norm(v, g, b):
    mu = jnp.mean(v, axis=-1, keepdims=True)
    d = v - mu
    var = jnp.mean(d * d, axis=-1, keepdims=True)
    return d * lax.rsqrt(var + LN_EPS) * g + b


def _resident(shape):
    nd = len(shape)
    return pl.BlockSpec(shape, lambda *_: (0,) * nd, pipeline_mode=pl.Buffered(1))


def _params(*sem):
    return pltpu.CompilerParams(dimension_semantics=sem, vmem_limit_bytes=V7X_VMEM_LIMIT_BYTES)


def _zero_after(v):
    bits = pltpu.bitcast(v[0:SUBLANES, :], jnp.uint32)
    zero = lax.shift_right_logical(lax.shift_right_logical(bits, jnp.uint32(16)), jnp.uint32(16))
    return pltpu.bitcast(zero, F32)[0:1, :]


def _dwconv_block(win_ref, wcs, w_ref, cs, r0, rb, halo, ktaps, after=None):
    pad = (ktaps - 1) // 2
    lo = (halo - pad) // SUBLANES * SUBLANES
    hi = -(-(halo + pad + rb) // SUBLANES) * SUBLANES
    n = hi - lo
    wv = win_ref[r0 + lo:r0 + lo + n, wcs]
    by_shift = {}
    for k in range(ktaps):
        q, r = divmod(halo - lo + k - pad, SUBLANES)
        by_shift.setdefault(r, []).append((k, q))
    acc = None
    for r, taps in by_shift.items():
        sh = wv if r == 0 else pltpu.roll(wv, n - r, axis=0)
        for k, q in taps:
            tap = w_ref[k:k + 1, cs] if after is None else w_ref[k:k + 1, cs] + after
            term = tap * sh[SUBLANES * q:SUBLANES * q + rb]
            acc = term if acc is None else acc + term
    return acc


def _emit_interleaved(mxu_items, vpu_items, chain):
    pending = list(vpu_items)
    last = None
    for i, item in enumerate(mxu_items):
        item()
        ready = [p for p in pending if p[0] <= i]
        slots_left = len(mxu_items) - i
        quota = -(-len(pending) // slots_left)
        for p in ready[:quota]:
            after = _zero_after(last) if chain and last is not None else None
            last = p[1](after)
            pending.remove(p)
    assert not pending


def _inproj_kernel(xp_ref, xm_ref, xn_ref, w_ref, cw_ref, cb_ref, u_ref, gates_ref, zs_ref, xc_ref, dt_ref, xw_ref,
                   *win_refs, nc, rb, halo, tiles_per_seq, conv_dim, d_model, d_inner, xbc_dim):
    tm = xm_ref.shape[0]
    pos = lax.rem(pl.program_id(0), tiles_per_seq)
    xh = jnp.concatenate([jnp.where(pos > 0, xp_ref[...], 0.0),
                          jnp.where(pos < tiles_per_seq - 1, xn_ref[...], 0.0)], axis=0)
    xw_ref[0:tm, :] = xm_ref[...].astype(BF16)
    xw_ref[tm:, :] = xh.astype(BF16)

    def mm(c0, n):
        return jnp.dot(xw_ref[0:tm, :], w_ref[:, c0:c0 + n], preferred_element_type=F32)

    off_gates = 2 * conv_dim
    off_z = off_gates + 2 * d_model
    off_xbc = off_z + d_inner
    off_dt = off_xbc + xbc_dim

    piece = lambda r: r[0:SUBLANES, 0:LANES]

    def xbc_chunk(k):
        def run():
            r = jnp.dot(xw_ref[...], w_ref[:, off_xbc + k * nc:off_xbc + (k + 1) * nc], preferred_element_type=F32)
            win = win_refs[k]
            win[0:halo, :] = r[tm:tm + halo]
            win[halo:halo + tm, :] = r[0:tm]
            win[halo + tm:, :] = r[tm + halo:]
            return piece(r)
        return run

    def conv_block(k, jl, r0):
        def run(after):
            wcs = slice(jl * LANES, (jl + 1) * LANES)
            cs = slice(k * nc + jl * LANES, k * nc + (jl + 1) * LANES)
            v = _dwconv_block(win_refs[k], wcs, cw_ref, cs, r0, rb, halo, SSM_CONV, after=after) + cb_ref[:, cs]
            xc_ref[r0:r0 + rb, cs] = (v * _sigmoid(v)).astype(BF16)
            return piece(v)
        return run

    def glu_chunk(k):
        def run():
            a = mm(k * nc, nc)
            g = mm(conv_dim + k * nc, nc)
            u_ref[:, k * nc:(k + 1) * nc] = (a * _sigmoid(g)).astype(BF16)
            return piece(g)
        return run

    def gate_chunk(k):
        def run():
            g = mm(off_gates + k * nc, nc)
            gates_ref[:, k * nc:(k + 1) * nc] = _sigmoid(g).astype(BF16)
            return piece(g)
        return run

    def z_chunk(k):
        def run():
            z = mm(off_z + k * nc, nc)
            zs_ref[:, k * nc:(k + 1) * nc] = (z * _sigmoid(z)).astype(BF16)
            return piece(z)
        return run

    def dt_chunk():
        r = mm(off_dt, LANES)
        dt_ref[...] = r
        return piece(r)

    mxu_items, vpu_items = [], []
    for k in range(xbc_dim // nc):
        mxu_items.append(xbc_chunk(k))
        for jl in range(nc // LANES):
            for r0 in range(0, tm, rb):
                vpu_items.append((k, conv_block(k, jl, r0)))
    mxu_items += [glu_chunk(k) for k in range(conv_dim // nc)]
    mxu_items += [gate_chunk(k) for k in range(2 * d_model // nc)]
    mxu_items += [z_chunk(k) for k in range(d_inner // nc)]
    mxu_items.append(dt_chunk)
    _emit_interleaved(mxu_items, vpu_items, chain=True)


def _inproj(x2d, w, cw, cb, *, s_len, conv_dim, d_model, d_inner, xbc_dim, tm=256, nc=512, rb=32):
    m, d = x2d.shape
    n_all = w.shape[1]
    halo = SUBLANES
    hb = tm // halo
    last = m // halo - 1
    kern = functools.partial(_inproj_kernel, nc=nc, rb=rb, halo=halo, tiles_per_seq=s_len // tm, conv_dim=conv_dim,
                             d_model=d_model, d_inner=d_inner, xbc_dim=xbc_dim)
    row = lambda i: (i, 0)
    return pl.pallas_call(
        kern,
        grid=(m // tm,),
        in_specs=[
            pl.BlockSpec((halo, d), lambda i: (jnp.maximum(i * hb - 1, 0), 0)),
            pl.BlockSpec((tm, d), row),
            pl.BlockSpec((halo, d), lambda i: (jnp.minimum((i + 1) * hb, last), 0)),
            _resident((d, n_all)), _resident(cw.shape), _resident(cb.shape),
        ],
        out_specs=[
            pl.BlockSpec((tm, conv_dim), row),
            pl.BlockSpec((tm, 2 * d_model), row),
            pl.BlockSpec((tm, d_inner), row),
            pl.BlockSpec((tm, xbc_dim), row),
            pl.BlockSpec((tm, LANES), row),
        ],
        out_shape=[
            jax.ShapeDtypeStruct((m, conv_dim), BF16),
            jax.ShapeDtypeStruct((m, 2 * d_model), BF16),
            jax.ShapeDtypeStruct((m, d_inner), BF16),
            jax.ShapeDtypeStruct((m, xbc_dim), BF16),
            jax.ShapeDtypeStruct((m, LANES), F32),
        ],
        scratch_shapes=[pltpu.VMEM((tm + 2 * halo, d), BF16)]
        + [pltpu.VMEM((tm + 2 * halo, nc), F32) for _ in range(xbc_dim // nc)],
        compiler_params=_params("arbitrary"),
        name="inproj",
    )(x2d, x2d, x2d, w, cw, cb)


def _expand_heads(v, e_ref):
    return jnp.dot(v.astype(BF16), e_ref[...], preferred_element_type=F32)


def _tri_masks():
    li = lax.broadcasted_iota(jnp.int32, (CHUNK, CHUNK), 0)
    si = lax.broadcasted_iota(jnp.int32, (CHUNK, CHUNK), 1)
    return li, si


def _dt_and_a(dt_raw, dtb_ref, alog_ref):
    dt = _softplus(dt_raw + dtb_ref[...])
    a = dt * (-jnp.exp(alog_ref[...]))
    return dt, a


def _group_slices(xc_ref, rows, g, d_inner):
    gw = d_inner // N_GROUPS
    x_g = xc_ref[rows, g * gw:(g + 1) * gw]
    b_g = xc_ref[rows, d_inner + g * D_STATE:d_inner + (g + 1) * D_STATE]
    c_g = xc_ref[rows, d_inner + (N_GROUPS + g) * D_STATE:d_inner + (N_GROUPS + g + 1) * D_STATE]
    return x_g, b_g, c_g


def _branch_a_ssd_bwd_kernel(up_ref, um_ref, un_ref, ga_ref, xc_ref, dt_ref, cw_ref, cb_ref, lg_ref, lb_ref, wo_ref,
                             dtb_ref, alog_ref, eb_ref, ya_ref, yoff_ref, win_ref, act_ref, state_ref, sb_ref, ob_ref,
                             dec_ref, *, t_rows, nt, rb, halo, d_inner):
    tt = pl.program_id(1)
    t = nt - 1 - tt

    @pl.when(tt == 0)
    def _():
        state_ref[...] = jnp.zeros_like(state_ref)

    c = um_ref.shape[-1]
    win_ref[0:halo, :] = jnp.where(t > 0, up_ref[...].astype(F32), 0.0)
    win_ref[halo:halo + t_rows, :] = um_ref[...].astype(F32)
    win_ref[halo + t_rows:, :] = jnp.where(t < nt - 1, un_ref[...].astype(F32), 0.0)
    def conv_block(j, r0):
        def run(after):
            cs = slice(j * LANES, (j + 1) * LANES)
            v = _dwconv_block(win_ref, cs, cw_ref, cs, r0, rb, halo, CONV_KERNEL, after=after) + cb_ref[:, cs]
            act_ref[r0:r0 + rb, cs] = v
            return v[0:SUBLANES, :]
        return run

    li, si = _tri_masks()
    triu = (si >= li).astype(F32)
    gw = d_inner // N_GROUPS
    nchunk = t_rows // CHUNK

    def decay_forms(ci):
        def run():
            rows = slice(ci * CHUNK, (ci + 1) * CHUNK)
            dt, a = _dt_and_a(dt_ref[rows, :], dtb_ref, alog_ref)
            rcs = jnp.dot(triu, a, precision=HI, preferred_element_type=F32)
            tot = rcs[0:1, :]
            sb_ref[rows, :] = _expand_heads(dt * jnp.exp(tot - rcs), eb_ref)
            ob_ref[rows, :] = _expand_heads(jnp.exp(rcs), eb_ref)
            dec_ref[ci * SUBLANES:(ci + 1) * SUBLANES, :] = _expand_heads(
                jnp.broadcast_to(jnp.exp(tot), (SUBLANES, LANES)), eb_ref)
            return rcs[0:SUBLANES, :]
        return run

    def group_body(ci, g):
        def run():
            rows = slice(ci * CHUNK, (ci + 1) * CHUNK)
            gs = slice(g * gw, (g + 1) * gw)
            x_g, b_g, c_g = _group_slices(xc_ref, rows, g, d_inner)
            st = state_ref[g]
            yoff_ref[rows, gs] = jnp.dot(c_g, st.astype(BF16), preferred_element_type=F32) * ob_ref[rows, gs]
            xdec = (x_g.astype(F32) * sb_ref[rows, gs]).astype(BF16)
            upd = lax.dot_general(b_g, xdec, (((0,), (0,)), ((), ())), preferred_element_type=F32)
            state_ref[g] = st * dec_ref[ci * SUBLANES:ci * SUBLANES + 1, gs] + upd
            return upd[0:SUBLANES, 0:LANES]
        return run

    mxu_items = [decay_forms(ci) for ci in range(nchunk)]
    mxu_items += [group_body(ci, g) for ci in reversed(range(nchunk)) for g in range(N_GROUPS)]
    vpu_items = [(0, conv_block(j, r0)) for j in range(c // LANES) for r0 in range(0, t_rows, rb)]
    _emit_interleaved(mxu_items, vpu_items, chain=False)

    v = _layer_norm(act_ref[...], lg_ref[...], lb_ref[...])
    s = (v * _sigmoid(v)).astype(BF16)
    ya = jnp.dot(s, wo_ref[...], preferred_element_type=F32)
    ya_ref[...] = (ya * ga_ref[...].astype(F32)).astype(BF16)


def _branch_a_ssd_bwd(u, gates, xc, dt_raw, cw, cb, lg, lb, wo, dtb, alog, eb, *, d_inner, t_rows=256, rb=32):
    b, s, c = u.shape
    d = wo.shape[1]
    xw = xc.shape[-1]
    nt = s // t_rows
    halo = BF16_ROWS
    hb = t_rows // halo
    last = s // halo - 1
    rev = lambda t: nt - 1 - t
    tile = lambda w: pl.BlockSpec((None, t_rows, w), lambda bi, t: (bi, rev(t), 0))
    kern = functools.partial(_branch_a_ssd_bwd_kernel, t_rows=t_rows, nt=nt, rb=rb, halo=halo, d_inner=d_inner)
    return pl.pallas_call(
        kern,
        grid=(b, nt),
        in_specs=[
            pl.BlockSpec((None, halo, c), lambda bi, t: (bi, jnp.maximum(rev(t) * hb - 1, 0), 0)),
            tile(c),
            pl.BlockSpec((None, halo, c), lambda bi, t: (bi, jnp.minimum((rev(t) + 1) * hb, last), 0)),
            tile(d),
            tile(xw), tile(LANES),
            _resident(cw.shape), _resident(cb.shape), _resident(lg.shape), _resident(lb.shape), _resident(wo.shape),
            _resident(dtb.shape), _resident(alog.shape), _resident(eb.shape),
        ],
        out_specs=[tile(d), tile(d_inner)],
        out_shape=[jax.ShapeDtypeStruct((b, s, d), BF16), jax.ShapeDtypeStruct((b, s, d_inner), F32)],
        scratch_shapes=[
            pltpu.VMEM((t_rows + 2 * halo, c), F32),
            pltpu.VMEM((t_rows, c), F32),
            pltpu.VMEM((N_GROUPS, D_STATE, d_inner // N_GROUPS), F32),
            pltpu.VMEM((t_rows, d_inner), F32),
            pltpu.VMEM((t_rows, d_inner), F32),
            pltpu.VMEM((t_rows // CHUNK * SUBLANES, d_inner), F32),
        ],
        compiler_params=_params("arbitrary", "arbitrary"),
        name="branch_a_ssd_bwd",
    )(u, u, u, gates, xc, dt_raw, cw, cb, lg, lb, wo, dtb, alog, eb)


def _ssd_fwd_kernel(xc_ref, dt_ref, yoffb_ref, zs_ref, ya_ref, gb_ref, dtb_ref, alog_ref, ef_ref, dsk_ref, ng_ref,
                    wbo_ref, o_ref, state_ref, yn_ref, cb_ref, w_ref, y_ref, sf_ref, of_ref, dec_ref,
                    *, t_rows, d_inner):
    t = pl.program_id(1)

    @pl.when(t == 0)
    def _():
        state_ref[...] = jnp.zeros_like(state_ref)

    li, si = _tri_masks()
    tril = (si <= li).astype(F32)
    lower = li > si
    upper = li < si
    lane_lo = si < HEAD_DIM
    gw = d_inner // N_GROUPS
    hpg = gw // HEAD_DIM
    nh = d_inner // HEAD_DIM
    nchunk = t_rows // CHUNK
    chunk_rows = [slice(ci * CHUNK, (ci + 1) * CHUNK) for ci in range(nchunk)]

    forms = []
    for ci, rows in enumerate(chunk_rows):
        dt, a = _dt_and_a(dt_ref[rows, :], dtb_ref, alog_ref)
        acs = jnp.dot(tril, a, precision=HI, preferred_element_type=F32)
        totf = acs[CHUNK - 1:CHUNK, :]
        rcs = totf - acs + a
        sf_ref[rows, :] = _expand_heads(dt * jnp.exp(totf - acs), ef_ref)
        of_ref[rows, :] = _expand_heads(jnp.exp(acs), ef_ref)
        dec_ref[ci * SUBLANES:(ci + 1) * SUBLANES, :] = _expand_heads(
            jnp.broadcast_to(jnp.exp(totf), (SUBLANES, LANES)), ef_ref)
        logdt = jnp.log(dt)
        dtT = dt.T
        forms.append((acs, rcs, (acs - logdt).T, (rcs - logdt).T, jnp.log(dtT[0:nh, :] + dtT[nh:2 * nh, :])))

    def cb_dot(ci, g):
        _, b_g, c_g = _group_slices(xc_ref, chunk_rows[ci], g, d_inner)
        cb_ref[ci, g] = lax.dot_general(c_g, b_g, (((1,), (1,)), ((), ())), preferred_element_type=F32)

    def head_matrix(ci, h):
        acs, rcs, rowf, rowb, drow = forms[ci]
        argf = acs[:, h:h + 1] - rowf[h:h + 1, :]
        argb = rcs[:, nh + h:nh + h + 1] - rowb[nh + h:nh + h + 1, :]
        arg = jnp.where(lower, argf, jnp.where(upper, argb, drow[h:h + 1, :]))
        w_ref[ci, :, h * CHUNK:(h + 1) * CHUNK] = (cb_ref[ci, h // hpg] * jnp.exp(arg)).astype(BF16)

    def pair_dot(ci, pr):
        rows = chunk_rows[ci]
        xp = xc_ref[rows, pr * LANES:(pr + 1) * LANES]
        zero = jnp.zeros_like(xp)
        xbd = jnp.concatenate([jnp.where(lane_lo, xp, zero), jnp.where(lane_lo, zero, xp)], axis=0)
        y_ref[rows, pr * LANES:(pr + 1) * LANES] = jnp.dot(
            w_ref[ci, :, 2 * pr * CHUNK:2 * (pr + 1) * CHUNK], xbd, preferred_element_type=F32)

    def group_tail(ci, g):
        rows = chunk_rows[ci]
        gs = slice(g * gw, (g + 1) * gw)
        x_g, b_g, c_g = _group_slices(xc_ref, rows, g, d_inner)
        st = state_ref[g]
        xf = x_g.astype(F32)
        yo = jnp.dot(c_g, st.astype(BF16), preferred_element_type=F32) * of_ref[rows, gs]
        y = y_ref[rows, gs] + yo + yoffb_ref[rows, gs] + xf * dsk_ref[:, gs]
        xdec = (xf * sf_ref[rows, gs]).astype(BF16)
        upd = lax.dot_general(b_g, xdec, (((0,), (0,)), ((), ())), preferred_element_type=F32)
        state_ref[g] = st * dec_ref[ci * SUBLANES:ci * SUBLANES + 1, gs] + upd
        yz = y * zs_ref[rows, gs].astype(F32)
        ms = jnp.mean(yz * yz, axis=-1, keepdims=True)
        yn_ref[rows, gs] = (yz * lax.rsqrt(ms + RMS_EPS) * ng_ref[:, gs]).astype(BF16)

    for ci in range(nchunk):
        for g in range(N_GROUPS):
            cb_dot(ci, g)
    for h in range(nh):
        head_matrix(0, h)
    for ci in range(nchunk):
        mxu_steps = []
        for g in range(N_GROUPS):
            mxu_steps += [lambda g=g, pr=pr: pair_dot(ci, g * hpg // 2 + pr) for pr in range(hpg // 2)]
            mxu_steps.append(lambda g=g: group_tail(ci, g))
        vpu_steps = [lambda h=h: head_matrix(ci + 1, h) for h in range(nh)] if ci + 1 < nchunk else []
        ratio = -(-len(vpu_steps) // len(mxu_steps))
        for j, step in enumerate(mxu_steps):
            for v in vpu_steps[j * ratio:(j + 1) * ratio]:
                v()
            step()

    yb = jnp.dot(yn_ref[...], wbo_ref[...], preferred_element_type=F32)
    o_ref[...] = (ya_ref[...].astype(F32) + gb_ref[...].astype(F32) * yb).astype(BF16)


def _ssd_fwd(xc, dt_raw, yoffb, zs, ya, gates, dtb, alog, ef, dsk, ng, wbo, *, d_inner, t_rows=512):
    b, s, c = xc.shape
    d = wbo.shape[1]
    nt = s // t_rows
    nchunk = t_rows // CHUNK
    nh = d_inner // HEAD_DIM
    kern = functools.partial(_ssd_fwd_kernel, t_rows=t_rows, d_inner=d_inner)
    tile = lambda w: pl.BlockSpec((None, t_rows, w), lambda bi, t: (bi, t, 0))
    return pl.pallas_call(
        kern,
        grid=(b, nt),
        in_specs=[
            tile(c), tile(LANES), tile(d_inner), tile(d_inner), tile(d),
            pl.BlockSpec((None, t_rows, d), lambda bi, t: (bi, t, 1)),
            _resident(dtb.shape), _resident(alog.shape), _resident(ef.shape), _resident(dsk.shape),
            _resident(ng.shape), _resident(wbo.shape),
        ],
        out_specs=tile(d),
        out_shape=jax.ShapeDtypeStruct((b, s, d), BF16),
        scratch_shapes=[
            pltpu.VMEM((N_GROUPS, D_STATE, d_inner // N_GROUPS), F32),
            pltpu.VMEM((t_rows, d_inner), BF16),
            pltpu.VMEM((nchunk, N_GROUPS, CHUNK, CHUNK), F32),
            pltpu.VMEM((nchunk, CHUNK, nh * CHUNK), BF16),
            pltpu.VMEM((t_rows, d_inner), F32),
            pltpu.VMEM((t_rows, d_inner), F32),
            pltpu.VMEM((t_rows, d_inner), F32),
            pltpu.VMEM((nchunk * SUBLANES, d_inner), F32),
        ],
        compiler_params=_params("arbitrary", "arbitrary"),
        name="ssd_fwd",
    )(xc, dt_raw, yoffb, zs, ya, gates, dtb, alog, ef, dsk, ng, wbo)


def _ffn_kernel(x_ref, m_ref, p_ref, wo_ref, l1g_ref, l1b_ref, wgu_ref, wd_ref, l2g_ref, l2b_ref, wple_ref,
                pg_ref, wpg_ref, o_ref, *, alpha, ffn, fc, n_sub):
    sub = x_ref.shape[0] // n_sub

    def stages(si):
        rows = slice(si * sub, (si + 1) * sub)
        mix = jnp.dot(m_ref[rows, :], wo_ref[...], preferred_element_type=F32)
        yield
        h = _layer_norm(alpha * x_ref[rows, :] + mix, l1g_ref[...], l1b_ref[...])
        hb = h.astype(BF16)
        yield
        acc = None
        for k in range(ffn // fc):
            g = jnp.dot(hb, wgu_ref[:, k * fc:(k + 1) * fc], preferred_element_type=F32)
            u = jnp.dot(hb, wgu_ref[:, ffn + k * fc:ffn + (k + 1) * fc], preferred_element_type=F32)
            act = (g * _sigmoid(g) * u).astype(BF16)
            part = jnp.dot(act, wd_ref[k * fc:(k + 1) * fc, :], preferred_element_type=F32)
            acc = part if acc is None else acc + part
            yield
        h2 = _layer_norm(alpha * h + acc, l2g_ref[...], l2b_ref[...])
        yield
        e = jnp.dot(p_ref[rows, :].astype(BF16), wple_ref[...], preferred_element_type=F32)
        e = e * lax.rsqrt(jnp.mean(e * e, axis=-1, keepdims=True) + RMS_EPS) * pg_ref[...]
        gate = _sigmoid(jnp.dot(h2.astype(BF16), wpg_ref[...], preferred_element_type=F32))
        o_ref[rows, :] = h2 + e * gate
        yield

    live = []
    for si in range(n_sub):
        live.append(stages(si))
        for gen in list(live):
            if next(gen, StopIteration) is StopIteration:
                live.remove(gen)
    while live:
        for gen in list(live):
            if next(gen, StopIteration) is StopIteration:
                live.remove(gen)


def _ffn(x2d, merged2d, p2d, wo, l1g, l1b, wgu, wd, l2g, l2b, wple, pg, wpg, *, alpha, tm=512, n_sub=2):
    m, d = x2d.shape
    ffn = wd.shape[0]
    fc = ffn // 2
    assert fc % LANES == 0
    kern = functools.partial(_ffn_kernel, alpha=alpha, ffn=ffn, fc=fc, n_sub=n_sub)
    row = lambda i: (i, 0)
    return pl.pallas_call(
        kern,
        grid=(m // tm,),
        in_specs=[
            pl.BlockSpec((tm, d), row), pl.BlockSpec((tm, d), row), pl.BlockSpec((tm, p2d.shape[1]), row),
            _resident(wo.shape), _resident(l1g.shape), _resident(l1b.shape), _resident(wgu.shape), _resident(wd.shape),
            _resident(l2g.shape), _resident(l2b.shape), _resident(wple.shape), _resident(pg.shape), _resident(wpg.shape),
        ],
        out_specs=pl.BlockSpec((tm, d), row),
        out_shape=jax.ShapeDtypeStruct((m, d), F32),
        compiler_params=_params("arbitrary"),
        name="ffn",
    )(x2d, merged2d, p2d, wo, l1g, l1b, wgu, wd, l2g, l2b, wple, pg, wpg)


def _head_expand_matrix(first_slot, n_heads, d_inner):
    slot = lax.broadcasted_iota(jnp.int32, (LANES, d_inner), 0)
    head = lax.broadcasted_iota(jnp.int32, (LANES, d_inner), 1) // HEAD_DIM
    return (slot == head + first_slot).astype(BF16)


def kernel(x, p, w_in, conv_a_w, conv_a_b, ln_a_g, ln_a_b, w_a_out, ssm_conv_w, ssm_conv_b, a_log, dt_bias, d_skip, ssm_norm_g, w_b_out, w_o, ln1_g, ln1_b, w_gate_up, w_down, ln2_g, ln2_b, w_ple, ple_norm_g, w_ple_gate):
    b, s, d_model = x.shape
    depth = w_in.shape[0]
    conv_dim = conv_a_w.shape[-1]
    d_inner = w_b_out.shape[1]
    xbc_dim = ssm_conv_w.shape[-1]
    n_heads = d_inner // HEAD_DIM
    m = b * s
    alpha = (2 * depth) ** 0.25
    n_main = 2 * conv_dim + 2 * d_model + d_inner + xbc_dim
    assert xbc_dim == d_inner + 2 * N_GROUPS * D_STATE and 2 * n_heads <= LANES and 2 * HEAD_DIM == LANES

    ef = _head_expand_matrix(0, n_heads, d_inner)
    eb = _head_expand_matrix(n_heads, n_heads, d_inner)
    row = lambda v: v.reshape(1, -1).astype(F32)
    slots = lambda v: jnp.pad(v.reshape(1, -1).astype(F32), ((0, 0), (0, LANES - 2 * n_heads)))

    x2d = x.reshape(m, d_model)
    for i in range(depth):
        w = jnp.pad(w_in[i], ((0, 0), (0, LANES - 2 * n_heads))).astype(BF16)
        assert w.shape[1] == n_main + LANES
        u, gates, zs, xc, dt_raw = _inproj(x2d, w, ssm_conv_w[i], row(ssm_conv_b[i]), s_len=s, conv_dim=conv_dim,
                                           d_model=d_model, d_inner=d_inner, xbc_dim=xbc_dim)
        dtb, alog = slots(dt_bias[i]), slots(a_log[i])
        dt3 = dt_raw.reshape(b, s, LANES)
        xc3 = xc.reshape(b, s, xbc_dim)
        gates3 = gates.reshape(b, s, 2 * d_model)
        ya, yoffb = _branch_a_ssd_bwd(u.reshape(b, s, conv_dim), gates3, xc3, dt3, conv_a_w[i], row(conv_a_b[i]),
                                      row(ln_a_g[i]), row(ln_a_b[i]), w_a_out[i].astype(BF16), dtb, alog, eb,
                                      d_inner=d_inner)
        merged = _ssd_fwd(xc3, dt3, yoffb, zs.reshape(b, s, d_inner), ya, gates3, dtb, alog, ef,
                          row(jnp.repeat(d_skip[i], HEAD_DIM)), row(ssm_norm_g[i]), w_b_out[i].astype(BF16),
                          d_inner=d_inner)
        x2d = _ffn(x2d, merged.reshape(m, d_model), p[i].reshape(m, -1), w_o[i].astype(BF16), row(ln1_g[i]), row(ln1_b[i]),
                   w_gate_up[i].astype(BF16), w_down[i].astype(BF16), row(ln2_g[i]), row(ln2_b[i]),
                   w_ple[i].astype(BF16), row(ple_norm_g[i]), w_ple_gate[i].astype(BF16), alpha=alpha)
    return x2d.reshape(b, s, d_model)
```

```python
import functools

import jax
import jax.numpy as jnp
from jax import lax
from jax.experimental import pallas as pl
from jax.experimental.pallas import tpu as pltpu

F32 = jnp.float32
BF16 = jnp.bfloat16

CONV_KERNEL = 31
SSM_CONV = 5
HEAD_DIM = 64
N_GROUPS = 8
D_STATE = 128
CHUNK = 128
LN_EPS = 1e-5
RMS_EPS = 1e-6

V7X_VMEM_LIMIT_BYTES = 56 * 1024 * 1024
SUBLANES = 8
LANES = 128
BF16_ROWS = 16

HI = lax.Precision.HIGHEST


def _sigmoid(v):
    return 1.0 / (1.0 + jnp.exp(-v))


def _softplus(v):
    return jnp.maximum(v, 0.0) + jnp.log1p(jnp.exp(-jnp.abs(v)))


def _layer_norm(v, g, b):
    mu = jnp.mean(v, axis=-1, keepdims=True)
    d = v - mu
    var = jnp.mean(d * d, axis=-1, keepdims=True)
    return d * lax.rsqrt(var + LN_EPS) * g + b


def _resident(shape):
    nd = len(shape)
    return pl.BlockSpec(shape, lambda *_: (0,) * nd, pipeline_mode=pl.Buffered(1))


def _params(*sem):
    return pltpu.CompilerParams(dimension_semantics=sem, vmem_limit_bytes=V7X_VMEM_LIMIT_BYTES)


def _zero_after(v):
    bits = pltpu.bitcast(v[0:SUBLANES, :], jnp.uint32)
    zero = lax.shift_right_logical(lax.shift_right_logical(bits, jnp.uint32(16)), jnp.uint32(16))
    return pltpu.bitcast(zero, F32)[0:1, :]


def _dwconv_block(win_ref, wcs, w_ref, cs, r0, rb, halo, ktaps, after=None):
    pad = (ktaps - 1) // 2
    lo = (halo - pad) // SUBLANES * SUBLANES
    hi = -(-(halo + pad + rb) // SUBLANES) * SUBLANES
    n = hi - lo
    wv = win_ref[r0 + lo:r0 + lo + n, wcs]
    by_shift = {}
    for k in range(ktaps):
        q, r = divmod(halo - lo + k - pad, SUBLANES)
        by_shift.setdefault(r, []).append((k, q))
    acc = None
    for r, taps in by_shift.items():
        sh = wv if r == 0 else pltpu.roll(wv, n - r, axis=0)
        for k, q in taps:
            tap = w_ref[k:k + 1, cs] if after is None else w_ref[k:k + 1, cs] + after
            term = tap * sh[SUBLANES * q:SUBLANES * q + rb]
            acc = term if acc is None else acc + term
    return acc


def _emit_interleaved(mxu_items, vpu_items, chain):
    pending = list(vpu_items)
    last = None
    for i, item in enumerate(mxu_items):
        item()
        ready = [p for p in pending if p[0] <= i]
        slots_left = len(mxu_items) - i
        quota = -(-len(pending) // slots_left)
        for p in ready[:quota]:
            after = _zero_after(last) if chain and last is not None else None
            last = p[1](after)
            pending.remove(p)
    assert not pending


def _inproj_kernel(xp_ref, xm_ref, xn_ref, w_ref, cw_ref, cb_ref, u_ref, gates_ref, zs_ref, xc_ref, dt_ref, xw_ref,
                   *win_refs, nc, rb, halo, tiles_per_seq, conv_dim, d_model, d_inner, xbc_dim):
    tm = xm_ref.shape[0]
    pos = lax.rem(pl.program_id(0), tiles_per_seq)
    xh = jnp.concatenate([jnp.where(pos > 0, xp_ref[...], 0.0),
                          jnp.where(pos < tiles_per_seq - 1, xn_ref[...], 0.0)], axis=0)
    xw_ref[0:tm, :] = xm_ref[...].astype(BF16)
    xw_ref[tm:, :] = xh.astype(BF16)

    def mm(c0, n):
        return jnp.dot(xw_ref[0:tm, :], w_ref[:, c0:c0 + n], preferred_element_type=F32)

    off_gates = 2 * conv_dim
    off_z = off_gates + 2 * d_model
    off_xbc = off_z + d_inner
    off_dt = off_xbc + xbc_dim

    piece = lambda r: r[0:SUBLANES, 0:LANES]

    def xbc_chunk(k):
        def run():
            r = jnp.dot(xw_ref[...], w_ref[:, off_xbc + k * nc:off_xbc + (k + 1) * nc], preferred_element_type=F32)
            win = win_refs[k]
            win[0:halo, :] = r[tm:tm + halo]
            win[halo:halo + tm, :] = r[0:tm]
            win[halo + tm:, :] = r[tm + halo:]
            return piece(r)
        return run

    def conv_block(k, jl, r0):
        def run(after):
            wcs = slice(jl * LANES, (jl + 1) * LANES)
            cs = slice(k * nc + jl * LANES, k * nc + (jl + 1) * LANES)
            v = _dwconv_block(win_refs[k], wcs, cw_ref, cs, r0, rb, halo, SSM_CONV, after=after) + cb_ref[:, cs]
            xc_ref[r0:r0 + rb, cs] = (v * _sigmoid(v)).astype(BF16)
            return piece(v)
        return run

    def glu_chunk(k):
        def run():
            a = mm(k * nc, nc)
            g = mm(conv_dim + k * nc, nc)
            u_ref[:, k * nc:(k + 1) * nc] = (a * _sigmoid(g)).astype(BF16)
            return piece(g)
        return run

    def gate_chunk(k):
        def run():
            g = mm(off_gates + k * nc, nc)
            gates_ref[:, k * nc:(k + 1) * nc] = _sigmoid(g).astype(BF16)
            return piece(g)
        return run

    def z_chunk(k):
        def run():
            z = mm(off_z + k * nc, nc)
            zs_ref[:, k * nc:(k + 1) * nc] = (z * _sigmoid(z)).astype(BF16)
            return piece(z)
        return run

    def dt_chunk():
        r = mm(off_dt, LANES)
        dt_ref[...] = r
        return piece(r)

    mxu_items, vpu_items = [], []
    for k in range(xbc_dim // nc):
        mxu_items.append(xbc_chunk(k))
        for jl in range(nc // LANES):
            for r0 in range(0, tm, rb):
                vpu_items.append((k, conv_block(k, jl, r0)))
    mxu_items += [glu_chunk(k) for k in range(conv_dim // nc)]
    mxu_items += [gate_chunk(k) for k in range(2 * d_model // nc)]
    mxu_items += [z_chunk(k) for k in range(d_inner // nc)]
    mxu_items.append(dt_chunk)
    _emit_interleaved(mxu_items, vpu_items, chain=True)


def _inproj(x2d, w, cw, cb, *, s_len, conv_dim, d_model, d_inner, xbc_dim, tm=256, nc=512, rb=32):
    m, d = x2d.shape
    n_all = w.shape[1]
    halo = SUBLANES
    hb = tm // halo
    last = m // halo - 1
    kern = functools.partial(_inproj_kernel, nc=nc, rb=rb, halo=halo, tiles_per_seq=s_len // tm, conv_dim=conv_dim,
                             d_model=d_model, d_inner=d_inner, xbc_dim=xbc_dim)
    row = lambda i: (i, 0)
    return pl.pallas_call(
        kern,
        grid=(m // tm,),
        in_specs=[
            pl.BlockSpec((halo, d), lambda i: (jnp.maximum(i * hb - 1, 0), 0)),
            pl.BlockSpec((tm, d), row),
            pl.BlockSpec((halo, d), lambda i: (jnp.minimum((i + 1) * hb, last), 0)),
            _resident((d, n_all)), _resident(cw.shape), _resident(cb.shape),
        ],
        out_specs=[
            pl.BlockSpec((tm, conv_dim), row),
            pl.BlockSpec((tm, 2 * d_model), row),
            pl.BlockSpec((tm, d_inner), row),
            pl.BlockSpec((tm, xbc_dim), row),
            pl.BlockSpec((tm, LANES), row),
        ],
        out_shape=[
            jax.ShapeDtypeStruct((m, conv_dim), BF16),
            jax.ShapeDtypeStruct((m, 2 * d_model), BF16),
            jax.ShapeDtypeStruct((m, d_inner), BF16),
            jax.ShapeDtypeStruct((m, xbc_dim), BF16),
            jax.ShapeDtypeStruct((m, LANES), F32),
        ],
        scratch_shapes=[pltpu.VMEM((tm + 2 * halo, d), BF16)]
        + [pltpu.VMEM((tm + 2 * halo, nc), F32) for _ in range(xbc_dim // nc)],
        compiler_params=_params("arbitrary"),
        name="inproj",
    )(x2d, x2d, x2d, w, cw, cb)


def _expand_heads(v, e_ref):
    return jnp.dot(v.astype(BF16), e_ref[...], preferred_element_type=F32)


def _tri_masks():
    li = lax.broadcasted_iota(jnp.int32, (CHUNK, CHUNK), 0)
    si = lax.broadcasted_iota(jnp.int32, (CHUNK, CHUNK), 1)
    return li, si


def _dt_and_a(dt_raw, dtb_ref, alog_ref):
    dt = _softplus(dt_raw + dtb_ref[...])
    a = dt * (-jnp.exp(alog_ref[...]))
    return dt, a


def _group_slices(xc_ref, rows, g, d_inner):
    gw = d_inner // N_GROUPS
    x_g = xc_ref[rows, g * gw:(g + 1) * gw]
    b_g = xc_ref[rows, d_inner + g * D_STATE:d_inner + (g + 1) * D_STATE]
    c_g = xc_ref[rows, d_inner + (N_GROUPS + g) * D_STATE:d_inner + (N_GROUPS + g + 1) * D_STATE]
    return x_g, b_g, c_g


def _branch_a_ssd_bwd_kernel(up_ref, um_ref, un_ref, ga_ref, xc_ref, dt_ref, cw_ref, cb_ref, lg_ref, lb_ref, wo_ref,
                             dtb_ref, alog_ref, eb_ref, ya_ref, yoff_ref, win_ref, act_ref, state_ref, sb_ref, ob_ref,
                             dec_ref, *, t_rows, nt, rb, halo, d_inner):
    tt = pl.program_id(1)
    t = nt - 1 - tt

    @pl.when(tt == 0)
    def _():
        state_ref[...] = jnp.zeros_like(state_ref)

    c = um_ref.shape[-1]
    win_ref[0:halo, :] = jnp.where(t > 0, up_ref[...].astype(F32), 0.0)
    win_ref[halo:halo + t_rows, :] = um_ref[...].astype(F32)
    win_ref[halo + t_rows:, :] = jnp.where(t < nt - 1, un_ref[...].astype(F32), 0.0)
    def conv_block(j, r0):
        def run(after):
            cs = slice(j * LANES, (j + 1) * LANES)
            v = _dwconv_block(win_ref, cs, cw_ref, cs, r0, rb, halo, CONV_KERNEL, after=after) + cb_ref[:, cs]
            act_ref[r0:r0 + rb, cs] = v
            return v[0:SUBLANES, :]
        return run

    li, si = _tri_masks()
    triu = (si >= li).astype(F32)
    gw = d_inner // N_GROUPS
    nchunk = t_rows // CHUNK

    def decay_forms(ci):
        def run():
            rows = slice(ci * CHUNK, (ci + 1) * CHUNK)
            dt, a = _dt_and_a(dt_ref[rows, :], dtb_ref, alog_ref)
            rcs = jnp.dot(triu, a, precision=HI, preferred_element_type=F32)
            tot = rcs[0:1, :]
            sb_ref[rows, :] = _expand_heads(dt * jnp.exp(tot - rcs), eb_ref)
            ob_ref[rows, :] = _expand_heads(jnp.exp(rcs), eb_ref)
            dec_ref[ci * SUBLANES:(ci + 1) * SUBLANES, :] = _expand_heads(
                jnp.broadcast_to(jnp.exp(tot), (SUBLANES, LANES)), eb_ref)
            return rcs[0:SUBLANES, :]
        return run

    def group_body(ci, g):
        def run():
            rows = slice(ci * CHUNK, (ci + 1) * CHUNK)
            gs = slice(g * gw, (g + 1) * gw)
            x_g, b_g, c_g = _group_slices(xc_ref, rows, g, d_inner)
            st = state_ref[g]
            yoff_ref[rows, gs] = jnp.dot(c_g, st.astype(BF16), preferred_element_type=F32) * ob_ref[rows, gs]
            xdec = (x_g.astype(F32) * sb_ref[rows, gs]).astype(BF16)
            upd = lax.dot_general(b_g, xdec, (((0,), (0,)), ((), ())), preferred_element_type=F32)
            state_ref[g] = st * dec_ref[ci * SUBLANES:ci * SUBLANES + 1, gs] + upd
            return upd[0:SUBLANES, 0:LANES]
        return run

    mxu_items = [decay_forms(ci) for ci in range(nchunk)]
    mxu_items += [group_body(ci, g) for ci in reversed(range(nchunk)) for g in range(N_GROUPS)]
    vpu_items = [(0, conv_block(j, r0)) for j in range(c // LANES) for r0 in range(0, t_rows, rb)]
    _emit_interleaved(mxu_items, vpu_items, chain=False)

    v = _layer_norm(act_ref[...], lg_ref[...], lb_ref[...])
    s = (v * _sigmoid(v)).astype(BF16)
    ya = jnp.dot(s, wo_ref[...], preferred_element_type=F32)
    ya_ref[...] = (ya * ga_ref[...].astype(F32)).astype(BF16)


def _branch_a_ssd_bwd(u, gates, xc, dt_raw, cw, cb, lg, lb, wo, dtb, alog, eb, *, d_inner, t_rows=512, rb=32):
    b, s, c = u.shape
    d = wo.shape[1]
    xw = xc.shape[-1]
    nt = s // t_rows
    halo = BF16_ROWS
    hb = t_rows // halo
    last = s // halo - 1
    rev = lambda t: nt - 1 - t
    tile = lambda w: pl.BlockSpec((None, t_rows, w), lambda bi, t: (bi, rev(t), 0))
    kern = functools.partial(_branch_a_ssd_bwd_kernel, t_rows=t_rows, nt=nt, rb=rb, halo=halo, d_inner=d_inner)
    return pl.pallas_call(
        kern,
        grid=(b, nt),
        in_specs=[
            pl.BlockSpec((None, halo, c), lambda bi, t: (bi, jnp.maximum(rev(t) * hb - 1, 0), 0)),
            tile(c),
            pl.BlockSpec((None, halo, c), lambda bi, t: (bi, jnp.minimum((rev(t) + 1) * hb, last), 0)),
            tile(d),
            tile(xw), tile(LANES),
            _resident(cw.shape), _resident(cb.shape), _resident(lg.shape), _resident(lb.shape), _resident(wo.shape),
            _resident(dtb.shape), _resident(alog.shape), _resident(eb.shape),
        ],
        out_specs=[tile(d), tile(d_inner)],
        out_shape=[jax.ShapeDtypeStruct((b, s, d), BF16), jax.ShapeDtypeStruct((b, s, d_inner), F32)],
        scratch_shapes=[
            pltpu.VMEM((t_rows + 2 * halo, c), F32),
            pltpu.VMEM((t_rows, c), F32),
            pltpu.VMEM((N_GROUPS, D_STATE, d_inner // N_GROUPS), F32),
            pltpu.VMEM((t_rows, d_inner), F32),
            pltpu.VMEM((t_rows, d_inner), F32),
            pltpu.VMEM((t_rows // CHUNK * SUBLANES, d_inner), F32),
        ],
        compiler_params=_params("arbitrary", "arbitrary"),
        name="branch_a_ssd_bwd",
    )(u, u, u, gates, xc, dt_raw, cw, cb, lg, lb, wo, dtb, alog, eb)


def _ssd_fwd_kernel(xc_ref, dt_ref, yoffb_ref, zs_ref, ya_ref, gb_ref, dtb_ref, alog_ref, ef_ref, dsk_ref, ng_ref,
                    wbo_ref, o_ref, state_ref, yn_ref, cb_ref, w_ref, y_ref, sf_ref, of_ref, dec_ref,
                    *, t_rows, d_inner):
    t = pl.program_id(1)

    @pl.when(t == 0)
    def _():
        state_ref[...] = jnp.zeros_like(state_ref)

    li, si = _tri_masks()
    tril = (si <= li).astype(F32)
    lower = li > si
    upper = li < si
    lane_lo = si < HEAD_DIM
    gw = d_inner // N_GROUPS
    hpg = gw // HEAD_DIM
    nh = d_inner // HEAD_DIM
    nchunk = t_rows // CHUNK
    chunk_rows = [slice(ci * CHUNK, (ci + 1) * CHUNK) for ci in range(nchunk)]

    forms = []
    for ci, rows in enumerate(chunk_rows):
        dt, a = _dt_and_a(dt_ref[rows, :], dtb_ref, alog_ref)
        acs = jnp.dot(tril, a, precision=HI, preferred_element_type=F32)
        totf = acs[CHUNK - 1:CHUNK, :]
        rcs = totf - acs + a
        sf_ref[rows, :] = _expand_heads(dt * jnp.exp(totf - acs), ef_ref)
        of_ref[rows, :] = _expand_heads(jnp.exp(acs), ef_ref)
        dec_ref[ci * SUBLANES:(ci + 1) * SUBLANES, :] = _expand_heads(
            jnp.broadcast_to(jnp.exp(totf), (SUBLANES, LANES)), ef_ref)
        logdt = jnp.log(dt)
        dtT = dt.T
        forms.append((acs, rcs, (acs - logdt).T, (rcs - logdt).T, jnp.log(dtT[0:nh, :] + dtT[nh:2 * nh, :])))

    def cb_dot(ci, g):
        _, b_g, c_g = _group_slices(xc_ref, chunk_rows[ci], g, d_inner)
        cb_ref[ci, g] = lax.dot_general(c_g, b_g, (((1,), (1,)), ((), ())), preferred_element_type=F32)

    def head_matrix(ci, h):
        acs, rcs, rowf, rowb, drow = forms[ci]
        argf = acs[:, h:h + 1] - rowf[h:h + 1, :]
        argb = rcs[:, nh + h:nh + h + 1] - rowb[nh + h:nh + h + 1, :]
        arg = jnp.where(lower, argf, jnp.where(upper, argb, drow[h:h + 1, :]))
        w_ref[ci, :, h * CHUNK:(h + 1) * CHUNK] = (cb_ref[ci, h // hpg] * jnp.exp(arg)).astype(BF16)

    def pair_dot(ci, pr):
        rows = chunk_rows[ci]
        xp = xc_ref[rows, pr * LANES:(pr + 1) * LANES]
        zero = jnp.zeros_like(xp)
        xbd = jnp.concatenate([jnp.where(lane_lo, xp, zero), jnp.where(lane_lo, zero, xp)], axis=0)
        y_ref[rows, pr * LANES:(pr + 1) * LANES] = jnp.dot(
            w_ref[ci, :, 2 * pr * CHUNK:2 * (pr + 1) * CHUNK], xbd, preferred_element_type=F32)

    def group_tail(ci, g):
        rows = chunk_rows[ci]
        gs = slice(g * gw, (g + 1) * gw)
        x_g, b_g, c_g = _group_slices(xc_ref, rows, g, d_inner)
        st = state_ref[g]
        xf = x_g.astype(F32)
        yo = jnp.dot(c_g, st.astype(BF16), preferred_element_type=F32) * of_ref[rows, gs]
        y = y_ref[rows, gs] + yo + yoffb_ref[rows, gs] + xf * dsk_ref[:, gs]
        xdec = (xf * sf_ref[rows, gs]).astype(BF16)
        upd = lax.dot_general(b_g, xdec, (((0,), (0,)), ((), ())), preferred_element_type=F32)
        state_ref[g] = st * dec_ref[ci * SUBLANES:ci * SUBLANES + 1, gs] + upd
        yz = y * zs_ref[rows, gs].astype(F32)
        ms = jnp.mean(yz * yz, axis=-1, keepdims=True)
        yn_ref[rows, gs] = (yz * lax.rsqrt(ms + RMS_EPS) * ng_ref[:, gs]).astype(BF16)

    for ci in range(nchunk):
        for g in range(N_GROUPS):
            cb_dot(ci, g)
    for h in range(nh):
        head_matrix(0, h)
    for ci in range(nchunk):
        mxu_steps = []
        for g in range(N_GROUPS):
            mxu_steps += [lambda g=g, pr=pr: pair_dot(ci, g * hpg // 2 + pr) for pr in range(hpg // 2)]
            mxu_steps.append(lambda g=g: group_tail(ci, g))
        vpu_steps = [lambda h=h: head_matrix(ci + 1, h) for h in range(nh)] if ci + 1 < nchunk else []
        ratio = -(-len(vpu_steps) // len(mxu_steps))
        for j, step in enumerate(mxu_steps):
            for v in vpu_steps[j * ratio:(j + 1) * ratio]:
                v()
            step()

    yb = jnp.dot(yn_ref[...], wbo_ref[...], preferred_element_type=F32)
    o_ref[...] = (ya_ref[...].astype(F32) + gb_ref[...].astype(F32) * yb).astype(BF16)


def _ssd_fwd(xc, dt_raw, yoffb, zs, ya, gates, dtb, alog, ef, dsk, ng, wbo, *, d_inner, t_rows=512):
    b, s, c = xc.shape
    d = wbo.shape[1]
    nt = s // t_rows
    nchunk = t_rows // CHUNK
    nh = d_inner // HEAD_DIM
    kern = functools.partial(_ssd_fwd_kernel, t_rows=t_rows, d_inner=d_inner)
    tile = lambda w: pl.BlockSpec((None, t_rows, w), lambda bi, t: (bi, t, 0))
    return pl.pallas_call(
        kern,
        grid=(b, nt),
        in_specs=[
            tile(c), tile(LANES), tile(d_inner), tile(d_inner), tile(d),
            pl.BlockSpec((None, t_rows, d), lambda bi, t: (bi, t, 1)),
            _resident(dtb.shape), _resident(alog.shape), _resident(ef.shape), _resident(dsk.shape),
            _resident(ng.shape), _resident(wbo.shape),
        ],
        out_specs=tile(d),
        out_shape=jax.ShapeDtypeStruct((b, s, d), BF16),
        scratch_shapes=[
            pltpu.VMEM((N_GROUPS, D_STATE, d_inner // N_GROUPS), F32),
            pltpu.VMEM((t_rows, d_inner), BF16),
            pltpu.VMEM((nchunk, N_GROUPS, CHUNK, CHUNK), F32),
            pltpu.VMEM((nchunk, CHUNK, nh * CHUNK), BF16),
            pltpu.VMEM((t_rows, d_inner), F32),
            pltpu.VMEM((t_rows, d_inner), F32),
            pltpu.VMEM((t_rows, d_inner), F32),
            pltpu.VMEM((nchunk * SUBLANES, d_inner), F32),
        ],
        compiler_params=_params("arbitrary", "arbitrary"),
        name="ssd_fwd",
    )(xc, dt_raw, yoffb, zs, ya, gates, dtb, alog, ef, dsk, ng, wbo)


def _ffn_kernel(x_ref, m_ref, p_ref, wo_ref, l1g_ref, l1b_ref, wgu_ref, wd_ref, l2g_ref, l2b_ref, wple_ref,
                pg_ref, wpg_ref, o_ref, *, alpha, ffn, fc, n_sub):
    sub = x_ref.shape[0] // n_sub

    def stages(si):
        rows = slice(si * sub, (si + 1) * sub)
        mix = jnp.dot(m_ref[rows, :], wo_ref[...], preferred_element_type=F32)
        yield
        h = _layer_norm(alpha * x_ref[rows, :] + mix, l1g_ref[...], l1b_ref[...])
        hb = h.astype(BF16)
        yield
        acc = None
        for k in range(ffn // fc):
            g = jnp.dot(hb, wgu_ref[:, k * fc:(k + 1) * fc], preferred_element_type=F32)
            u = jnp.dot(hb, wgu_ref[:, ffn + k * fc:ffn + (k + 1) * fc], preferred_element_type=F32)
            act = (g * _sigmoid(g) * u).astype(BF16)
            part = jnp.dot(act, wd_ref[k * fc:(k + 1) * fc, :], preferred_element_type=F32)
            acc = part if acc is None else acc + part
            yield
        h2 = _layer_norm(alpha * h + acc, l2g_ref[...], l2b_ref[...])
        yield
        e = jnp.dot(p_ref[rows, :].astype(BF16), wple_ref[...], preferred_element_type=F32)
        e = e * lax.rsqrt(jnp.mean(e * e, axis=-1, keepdims=True) + RMS_EPS) * pg_ref[...]
        gate = _sigmoid(jnp.dot(h2.astype(BF16), wpg_ref[...], preferred_element_type=F32))
        o_ref[rows, :] = h2 + e * gate
        yield

    live = []
    for si in range(n_sub):
        live.append(stages(si))
        for gen in list(live):
            if next(gen, StopIteration) is StopIteration:
                live.remove(gen)
    while live:
        for gen in list(live):
            if next(gen, StopIteration) is StopIteration:
                live.remove(gen)


def _ffn(x2d, merged2d, p2d, wo, l1g, l1b, wgu, wd, l2g, l2b, wple, pg, wpg, *, alpha, tm=512, n_sub=2):
    m, d = x2d.shape
    ffn = wd.shape[0]
    fc = ffn // 2
    assert fc % LANES == 0
    kern = functools.partial(_ffn_kernel, alpha=alpha, ffn=ffn, fc=fc, n_sub=n_sub)
    row = lambda i: (i, 0)
    return pl.pallas_call(
        kern,
        grid=(m // tm,),
        in_specs=[
            pl.BlockSpec((tm, d), row), pl.BlockSpec((tm, d), row), pl.BlockSpec((tm, p2d.shape[1]), row),
            _resident(wo.shape), _resident(l1g.shape), _resident(l1b.shape), _resident(wgu.shape), _resident(wd.shape),
            _resident(l2g.shape), _resident(l2b.shape), _resident(wple.shape), _resident(pg.shape), _resident(wpg.shape),
        ],
        out_specs=pl.BlockSpec((tm, d), row),
        out_shape=jax.ShapeDtypeStruct((m, d), F32),
        compiler_params=_params("arbitrary"),
        name="ffn",
    )(x2d, merged2d, p2d, wo, l1g, l1b, wgu, wd, l2g, l2b, wple, pg, wpg)


def _head_expand_matrix(first_slot, n_heads, d_inner):
    slot = lax.broadcasted_iota(jnp.int32, (LANES, d_inner), 0)
    head = lax.broadcasted_iota(jnp.int32, (LANES, d_inner), 1) // HEAD_DIM
    return (slot == head + first_slot).astype(BF16)


def kernel(x, p, w_in, conv_a_w, conv_a_b, ln_a_g, ln_a_b, w_a_out, ssm_conv_w, ssm_conv_b, a_log, dt_bias, d_skip, ssm_norm_g, w_b_out, w_o, ln1_g, ln1_b, w_gate_up, w_down, ln2_g, ln2_b, w_ple, ple_norm_g, w_ple_gate):
    b, s, d_model = x.shape
    depth = w_in.shape[0]
    conv_dim = conv_a_w.shape[-1]
    d_inner = w_b_out.shape[1]
    xbc_dim = ssm_conv_w.shape[-1]
    n_heads = d_inner // HEAD_DIM
    m = b * s
    alpha = (2 * depth) ** 0.25
    n_main = 2 * conv_dim + 2 * d_model + d_inner + xbc_dim
    assert xbc_dim == d_inner + 2 * N_GROUPS * D_STATE and 2 * n_heads <= LANES and 2 * HEAD_DIM == LANES

    ef = _head_expand_matrix(0, n_heads, d_inner)
    eb = _head_expand_matrix(n_heads, n_heads, d_inner)
    row = lambda v: v.reshape(1, -1).astype(F32)
    slots = lambda v: jnp.pad(v.reshape(1, -1).astype(F32), ((0, 0), (0, LANES - 2 * n_heads)))

    x2d = x.reshape(m, d_model)
    for i in range(depth):
        w = jnp.pad(w_in[i], ((0, 0), (0, LANES - 2 * n_heads))).astype(BF16)
        assert w.shape[1] == n_main + LANES
        u, gates, zs, xc, dt_raw = _inproj(x2d, w, ssm_conv_w[i], row(ssm_conv_b[i]), s_len=s, conv_dim=conv_dim,
                                           d_model=d_model, d_inner=d_inner, xbc_dim=xbc_dim)
        dtb, alog = slots(dt_bias[i]), slots(a_log[i])
        dt3 = dt_raw.reshape(b, s, LANES)
        xc3 = xc.reshape(b, s, xbc_dim)
        gates3 = gates.reshape(b, s, 2 * d_model)
        ya, yoffb = _branch_a_ssd_bwd(u.reshape(b, s, conv_dim), gates3, xc3, dt3, conv_a_w[i], row(conv_a_b[i]),
                                      row(ln_a_g[i]), row(ln_a_b[i]), w_a_out[i].astype(BF16), dtb, alog, eb,
                                      d_inner=d_inner)
        merged = _ssd_fwd(xc3, dt3, yoffb, zs.reshape(b, s, d_inner), ya, gates3, dtb, alog, ef,
                          row(jnp.repeat(d_skip[i], HEAD_DIM)), row(ssm_norm_g[i]), w_b_out[i].astype(BF16),
                          d_inner=d_inner)
        x2d = _ffn(x2d, merged.reshape(m, d_model), p[i].reshape(m, -1), w_o[i].astype(BF16), row(ln1_g[i]), row(ln1_b[i]),
                   w_gate_up[i].astype(BF16), w_down[i].astype(BF16), row(ln2_g[i]), row(ln2_b[i]),
                   w_ple[i].astype(BF16), row(ple_norm_g[i]), w_ple_gate[i].astype(BF16), alpha=alpha)
    return x2d.reshape(b, s, d_model)
```

```python
import functools

import jax
import jax.numpy as jnp
from jax import lax
from jax.experimental import pallas as pl
from jax.experimental.pallas import tpu as pltpu

F32 = jnp.float32
BF16 = jnp.bfloat16

CONV_KERNEL = 31
SSM_CONV = 5
HEAD_DIM = 64
N_GROUPS = 8
D_STATE = 128
CHUNK = 128
LN_EPS = 1e-5
RMS_EPS = 1e-6

V7X_VMEM_LIMIT_BYTES = 56 * 1024 * 1024
SUBLANES = 8
LANES = 128
BF16_ROWS = 16

HI = lax.Precision.HIGHEST


def _sigmoid(v):
    return 1.0 / (1.0 + jnp.exp(-v))


def _softplus(v):
    return jnp.maximum(v, 0.0) + jnp.log1p(jnp.exp(-jnp.abs(v)))


def _layer_norm(v, g, b):
    mu = jnp.mean(v, axis=-1, keepdims=True)
    d = v - mu
    var = jnp.mean(d * d, axis=-1, keepdims=True)
    return d * lax.rsqrt(var + LN_EPS) * g + b


def _resident(shape):
    nd = len(shape)
    return pl.BlockSpec(shape, lambda *_: (0,) * nd, pipeline_mode=pl.Buffered(1))


def _resident_layer(stacked, layer):
    shape = stacked.shape[1:]
    return pl.BlockSpec((None,) + shape, lambda *_: (layer,) + (0,) * len(shape), pipeline_mode=pl.Buffered(1))


def _params(*sem):
    return pltpu.CompilerParams(dimension_semantics=sem, vmem_limit_bytes=V7X_VMEM_LIMIT_BYTES)


def _zero_after(v):
    bits = pltpu.bitcast(v[0:SUBLANES, :], jnp.uint32)
    zero = lax.shift_right_logical(lax.shift_right_logical(bits, jnp.uint32(16)), jnp.uint32(16))
    return pltpu.bitcast(zero, F32)[0:1, :]


def _dwconv_block(win_ref, wcs, w_ref, cs, r0, rb, halo, ktaps, after=None):
    pad = (ktaps - 1) // 2
    lo = (halo - pad) // SUBLANES * SUBLANES
    hi = -(-(halo + pad + rb) // SUBLANES) * SUBLANES
    n = hi - lo
    wv = win_ref[r0 + lo:r0 + lo + n, wcs]
    by_shift = {}
    for k in range(ktaps):
        q, r = divmod(halo - lo + k - pad, SUBLANES)
        by_shift.setdefault(r, []).append((k, q))
    acc = None
    for r, taps in by_shift.items():
        sh = wv if r == 0 else pltpu.roll(wv, n - r, axis=0)
        for k, q in taps:
            tap = w_ref[k:k + 1, cs] if after is None else w_ref[k:k + 1, cs] + after
            term = tap * sh[SUBLANES * q:SUBLANES * q + rb]
            acc = term if acc is None else acc + term
    return acc


def _emit_interleaved(mxu_items, vpu_items, chain):
    pending = list(vpu_items)
    last = None
    for i, item in enumerate(mxu_items):
        item()
        ready = [p for p in pending if p[0] <= i]
        slots_left = len(mxu_items) - i
        quota = -(-len(pending) // slots_left)
        for p in ready[:quota]:
            after = _zero_after(last) if chain and last is not None else None
            last = p[1](after)
            pending.remove(p)
    assert not pending


def _inproj_kernel(xp_ref, xm_ref, xn_ref, w_ref, cw_ref, cb_ref, u_ref, gates_ref, zs_ref, xc_ref, dt_ref, xw_ref,
                   *win_refs, nc, rb, halo, tiles_per_seq, conv_dim, d_model, d_inner, xbc_dim):
    tm = xm_ref.shape[0]
    pos = lax.rem(pl.program_id(0), tiles_per_seq)
    xh = jnp.concatenate([jnp.where(pos > 0, xp_ref[...], 0.0),
                          jnp.where(pos < tiles_per_seq - 1, xn_ref[...], 0.0)], axis=0)
    xw_ref[0:tm, :] = xm_ref[...].astype(BF16)
    xw_ref[tm:, :] = xh.astype(BF16)

    def mm(c0, n):
        return jnp.dot(xw_ref[0:tm, :], w_ref[:, c0:c0 + n], preferred_element_type=F32)

    off_gates = 2 * conv_dim
    off_z = off_gates + 2 * d_model
    off_xbc = off_z + d_inner
    off_dt = off_xbc + xbc_dim

    piece = lambda r: r[0:SUBLANES, 0:LANES]

    def xbc_chunk(k):
        def run():
            r = jnp.dot(xw_ref[...], w_ref[:, off_xbc + k * nc:off_xbc + (k + 1) * nc], preferred_element_type=F32)
            win = win_refs[k]
            win[0:halo, :] = r[tm:tm + halo]
            win[halo:halo + tm, :] = r[0:tm]
            win[halo + tm:, :] = r[tm + halo:]
            return piece(r)
        return run

    def conv_block(k, jl, r0):
        def run(after):
            wcs = slice(jl * LANES, (jl + 1) * LANES)
            cs = slice(k * nc + jl * LANES, k * nc + (jl + 1) * LANES)
            v = _dwconv_block(win_refs[k], wcs, cw_ref, cs, r0, rb, halo, SSM_CONV, after=after) + cb_ref[:, cs]
            xc_ref[r0:r0 + rb, cs] = (v * _sigmoid(v)).astype(BF16)
            return piece(v)
        return run

    def glu_chunk(k):
        def run():
            a = mm(k * nc, nc)
            g = mm(conv_dim + k * nc, nc)
            u_ref[:, k * nc:(k + 1) * nc] = (a * _sigmoid(g)).astype(BF16)
            return piece(g)
        return run

    def gate_chunk(k):
        def run():
            g = mm(off_gates + k * nc, nc)
            gates_ref[:, k * nc:(k + 1) * nc] = _sigmoid(g).astype(BF16)
            return piece(g)
        return run

    def z_chunk(k):
        def run():
            z = mm(off_z + k * nc, nc)
            zs_ref[:, k * nc:(k + 1) * nc] = (z * _sigmoid(z)).astype(BF16)
            return piece(z)
        return run

    def dt_chunk():
        n_dt = w_ref.shape[1] - off_dt
        dt_ref[:, 0:n_dt] = mm(off_dt, n_dt)
        dt_ref[:, n_dt:] = jnp.zeros((tm, LANES - n_dt), F32)

    mxu_items, vpu_items = [], []
    for k in range(xbc_dim // nc):
        mxu_items.append(xbc_chunk(k))
        for jl in range(nc // LANES):
            for r0 in range(0, tm, rb):
                vpu_items.append((k, conv_block(k, jl, r0)))
    mxu_items += [glu_chunk(k) for k in range(conv_dim // nc)]
    mxu_items += [gate_chunk(k) for k in range(2 * d_model // nc)]
    mxu_items += [z_chunk(k) for k in range(d_inner // nc)]
    mxu_items.append(dt_chunk)
    _emit_interleaved(mxu_items, vpu_items, chain=True)


def _inproj(x2d, w_all, layer, cw, cb, *, s_len, conv_dim, d_model, d_inner, xbc_dim, tm=256, nc=512, rb=32):
    m, d = x2d.shape
    n_all = w_all.shape[2]
    halo = SUBLANES
    hb = tm // halo
    last = m // halo - 1
    kern = functools.partial(_inproj_kernel, nc=nc, rb=rb, halo=halo, tiles_per_seq=s_len // tm, conv_dim=conv_dim,
                             d_model=d_model, d_inner=d_inner, xbc_dim=xbc_dim)
    row = lambda i: (i, 0)
    return pl.pallas_call(
        kern,
        grid=(m // tm,),
        in_specs=[
            pl.BlockSpec((halo, d), lambda i: (jnp.maximum(i * hb - 1, 0), 0)),
            pl.BlockSpec((tm, d), row),
            pl.BlockSpec((halo, d), lambda i: (jnp.minimum((i + 1) * hb, last), 0)),
            pl.BlockSpec((None, d, n_all), lambda i: (layer, 0, 0), pipeline_mode=pl.Buffered(1)),
            _resident(cw.shape), _resident(cb.shape),
        ],
        out_specs=[
            pl.BlockSpec((tm, conv_dim), row),
            pl.BlockSpec((tm, 2 * d_model), row),
            pl.BlockSpec((tm, d_inner), row),
            pl.BlockSpec((tm, xbc_dim), row),
            pl.BlockSpec((tm, LANES), row),
        ],
        out_shape=[
            jax.ShapeDtypeStruct((m, conv_dim), BF16),
            jax.ShapeDtypeStruct((m, 2 * d_model), BF16),
            jax.ShapeDtypeStruct((m, d_inner), BF16),
            jax.ShapeDtypeStruct((m, xbc_dim), BF16),
            jax.ShapeDtypeStruct((m, LANES), F32),
        ],
        scratch_shapes=[pltpu.VMEM((tm + 2 * halo, d), BF16)]
        + [pltpu.VMEM((tm + 2 * halo, nc), F32) for _ in range(xbc_dim // nc)],
        compiler_params=_params("arbitrary"),
        name="inproj",
    )(x2d, x2d, x2d, w_all, cw, cb)


def _expand_heads(v, e_ref):
    return jnp.dot(v.astype(BF16), e_ref[...], preferred_element_type=F32)


def _tri_masks():
    li = lax.broadcasted_iota(jnp.int32, (CHUNK, CHUNK), 0)
    si = lax.broadcasted_iota(jnp.int32, (CHUNK, CHUNK), 1)
    return li, si


def _dt_and_a(dt_raw, dtb_ref, alog_ref):
    dt = _softplus(dt_raw + dtb_ref[...])
    a = dt * (-jnp.exp(alog_ref[...]))
    return dt, a


def _group_slices(xc_ref, rows, g, d_inner):
    gw = d_inner // N_GROUPS
    x_g = xc_ref[rows, g * gw:(g + 1) * gw]
    b_g = xc_ref[rows, d_inner + g * D_STATE:d_inner + (g + 1) * D_STATE]
    c_g = xc_ref[rows, d_inner + (N_GROUPS + g) * D_STATE:d_inner + (N_GROUPS + g + 1) * D_STATE]
    return x_g, b_g, c_g


def _branch_a_ssd_bwd_kernel(up_ref, um_ref, un_ref, ga_ref, xc_ref, dt_ref, cw_ref, cb_ref, lg_ref, lb_ref, wo_ref,
                             dtb_ref, alog_ref, eb_ref, ya_ref, yoff_ref, win_ref, act_ref, state_ref, sb_ref, ob_ref,
                             dec_ref, *, t_rows, nt, rb, halo, d_inner):
    tt = pl.program_id(1)
    t = nt - 1 - tt

    @pl.when(tt == 0)
    def _():
        state_ref[...] = jnp.zeros_like(state_ref)

    c = um_ref.shape[-1]
    win_ref[0:halo, :] = jnp.where(t > 0, up_ref[...].astype(F32), 0.0)
    win_ref[halo:halo + t_rows, :] = um_ref[...].astype(F32)
    win_ref[halo + t_rows:, :] = jnp.where(t < nt - 1, un_ref[...].astype(F32), 0.0)
    def conv_block(j, r0):
        def run(after):
            cs = slice(j * LANES, (j + 1) * LANES)
            v = _dwconv_block(win_ref, cs, cw_ref, cs, r0, rb, halo, CONV_KERNEL, after=after) + cb_ref[:, cs]
            act_ref[r0:r0 + rb, cs] = v
            return v[0:SUBLANES, :]
        return run

    li, si = _tri_masks()
    triu = (si >= li).astype(F32)
    gw = d_inner // N_GROUPS
    nchunk = t_rows // CHUNK

    def decay_forms(ci):
        def run():
            rows = slice(ci * CHUNK, (ci + 1) * CHUNK)
            dt, a = _dt_and_a(dt_ref[rows, :], dtb_ref, alog_ref)
            rcs = jnp.dot(triu, a, precision=HI, preferred_element_type=F32)
            tot = rcs[0:1, :]
            sb_ref[rows, :] = _expand_heads(dt * jnp.exp(tot - rcs), eb_ref)
            ob_ref[rows, :] = _expand_heads(jnp.exp(rcs), eb_ref)
            dec_ref[ci * SUBLANES:(ci + 1) * SUBLANES, :] = _expand_heads(
                jnp.broadcast_to(jnp.exp(tot), (SUBLANES, LANES)), eb_ref)
            return rcs[0:SUBLANES, :]
        return run

    def group_body(ci, g):
        def run():
            rows = slice(ci * CHUNK, (ci + 1) * CHUNK)
            gs = slice(g * gw, (g + 1) * gw)
            x_g, b_g, c_g = _group_slices(xc_ref, rows, g, d_inner)
            st = state_ref[g]
            yoff_ref[rows, gs] = jnp.dot(c_g, st.astype(BF16), preferred_element_type=F32) * ob_ref[rows, gs]
            xdec = (x_g.astype(F32) * sb_ref[rows, gs]).astype(BF16)
            upd = lax.dot_general(b_g, xdec, (((0,), (0,)), ((), ())), preferred_element_type=F32)
            state_ref[g] = st * dec_ref[ci * SUBLANES:ci * SUBLANES + 1, gs] + upd
            return upd[0:SUBLANES, 0:LANES]
        return run

    mxu_items = [decay_forms(ci) for ci in range(nchunk)]
    mxu_items += [group_body(ci, g) for ci in reversed(range(nchunk)) for g in range(N_GROUPS)]
    vpu_items = [(0, conv_block(j, r0)) for j in range(c // LANES) for r0 in range(0, t_rows, rb)]
    _emit_interleaved(mxu_items, vpu_items, chain=False)

    v = _layer_norm(act_ref[...], lg_ref[...], lb_ref[...])
    s = (v * _sigmoid(v)).astype(BF16)
    ya = jnp.dot(s, wo_ref[...], preferred_element_type=F32)
    ya_ref[...] = (ya * ga_ref[...].astype(F32)).astype(BF16)


def _branch_a_ssd_bwd(u, gates, xc, dt_raw, cw, cb, lg, lb, wo, dtb, alog, eb, *, d_inner, t_rows=512, rb=32):
    b, s, c = u.shape
    d = wo.shape[1]
    xw = xc.shape[-1]
    nt = s // t_rows
    halo = BF16_ROWS
    hb = t_rows // halo
    last = s // halo - 1
    rev = lambda t: nt - 1 - t
    tile = lambda w: pl.BlockSpec((None, t_rows, w), lambda bi, t: (bi, rev(t), 0))
    kern = functools.partial(_branch_a_ssd_bwd_kernel, t_rows=t_rows, nt=nt, rb=rb, halo=halo, d_inner=d_inner)
    return pl.pallas_call(
        kern,
        grid=(b, nt),
        in_specs=[
            pl.BlockSpec((None, halo, c), lambda bi, t: (bi, jnp.maximum(rev(t) * hb - 1, 0), 0)),
            tile(c),
            pl.BlockSpec((None, halo, c), lambda bi, t: (bi, jnp.minimum((rev(t) + 1) * hb, last), 0)),
            tile(d),
            tile(xw), tile(LANES),
            _resident(cw.shape), _resident(cb.shape), _resident(lg.shape), _resident(lb.shape), _resident(wo.shape),
            _resident(dtb.shape), _resident(alog.shape), _resident(eb.shape),
        ],
        out_specs=[tile(d), tile(d_inner)],
        out_shape=[jax.ShapeDtypeStruct((b, s, d), BF16), jax.ShapeDtypeStruct((b, s, d_inner), F32)],
        scratch_shapes=[
            pltpu.VMEM((t_rows + 2 * halo, c), F32),
            pltpu.VMEM((t_rows, c), F32),
            pltpu.VMEM((N_GROUPS, D_STATE, d_inner // N_GROUPS), F32),
            pltpu.VMEM((t_rows, d_inner), F32),
            pltpu.VMEM((t_rows, d_inner), F32),
            pltpu.VMEM((t_rows // CHUNK * SUBLANES, d_inner), F32),
        ],
        compiler_params=_params("arbitrary", "arbitrary"),
        name="branch_a_ssd_bwd",
    )(u, u, u, gates, xc, dt_raw, cw, cb, lg, lb, wo, dtb, alog, eb)


def _ssd_fwd_kernel(xc_ref, dt_ref, yoffb_ref, zs_ref, ya_ref, gb_ref, dtb_ref, alog_ref, ef_ref, dsk_ref, ng_ref,
                    wbo_ref, o_ref, state_ref, yn_ref, cb_ref, w_ref, y_ref, sf_ref, of_ref, dec_ref,
                    *, t_rows, d_inner):
    t = pl.program_id(1)

    @pl.when(t == 0)
    def _():
        state_ref[...] = jnp.zeros_like(state_ref)

    li, si = _tri_masks()
    tril = (si <= li).astype(F32)
    lower = li > si
    upper = li < si
    lane_lo = si < HEAD_DIM
    gw = d_inner // N_GROUPS
    hpg = gw // HEAD_DIM
    nh = d_inner // HEAD_DIM
    nchunk = t_rows // CHUNK
    chunk_rows = [slice(ci * CHUNK, (ci + 1) * CHUNK) for ci in range(nchunk)]

    forms = []
    for ci, rows in enumerate(chunk_rows):
        dt, a = _dt_and_a(dt_ref[rows, :], dtb_ref, alog_ref)
        acs = jnp.dot(tril, a, precision=HI, preferred_element_type=F32)
        totf = acs[CHUNK - 1:CHUNK, :]
        rcs = totf - acs + a
        sf_ref[rows, :] = _expand_heads(dt * jnp.exp(totf - acs), ef_ref)
        of_ref[rows, :] = _expand_heads(jnp.exp(acs), ef_ref)
        dec_ref[ci * SUBLANES:(ci + 1) * SUBLANES, :] = _expand_heads(
            jnp.broadcast_to(jnp.exp(totf), (SUBLANES, LANES)), ef_ref)
        logdt = jnp.log(dt)
        dtT = dt.T
        forms.append((acs, rcs, (acs - logdt).T, (rcs - logdt).T, jnp.log(dtT[0:nh, :] + dtT[nh:2 * nh, :])))

    def cb_dot(ci, g):
        _, b_g, c_g = _group_slices(xc_ref, chunk_rows[ci], g, d_inner)
        cb_ref[ci, g] = lax.dot_general(c_g, b_g, (((1,), (1,)), ((), ())), preferred_element_type=F32)

    def head_matrix(ci, h):
        acs, rcs, rowf, rowb, drow = forms[ci]
        argf = acs[:, h:h + 1] - rowf[h:h + 1, :]
        argb = rcs[:, nh + h:nh + h + 1] - rowb[nh + h:nh + h + 1, :]
        arg = jnp.where(lower, argf, jnp.where(upper, argb, drow[h:h + 1, :]))
        w_ref[ci, :, h * CHUNK:(h + 1) * CHUNK] = (cb_ref[ci, h // hpg] * jnp.exp(arg)).astype(BF16)

    def pair_dot(ci, pr):
        rows = chunk_rows[ci]
        xp = xc_ref[rows, pr * LANES:(pr + 1) * LANES]
        zero = jnp.zeros_like(xp)
        xbd = jnp.concatenate([jnp.where(lane_lo, xp, zero), jnp.where(lane_lo, zero, xp)], axis=0)
        y_ref[rows, pr * LANES:(pr + 1) * LANES] = jnp.dot(
            w_ref[ci, :, 2 * pr * CHUNK:2 * (pr + 1) * CHUNK], xbd, preferred_element_type=F32)

    def group_tail(ci, g):
        rows = chunk_rows[ci]
        gs = slice(g * gw, (g + 1) * gw)
        x_g, b_g, c_g = _group_slices(xc_ref, rows, g, d_inner)
        st = state_ref[g]
        xf = x_g.astype(F32)
        yo = jnp.dot(c_g, st.astype(BF16), preferred_element_type=F32) * of_ref[rows, gs]
        y = y_ref[rows, gs] + yo + yoffb_ref[rows, gs] + xf * dsk_ref[:, gs]
        xdec = (xf * sf_ref[rows, gs]).astype(BF16)
        upd = lax.dot_general(b_g, xdec, (((0,), (0,)), ((), ())), preferred_element_type=F32)
        state_ref[g] = st * dec_ref[ci * SUBLANES:ci * SUBLANES + 1, gs] + upd
        yz = y * zs_ref[rows, gs].astype(F32)
        ms = jnp.mean(yz * yz, axis=-1, keepdims=True)
        yn_ref[rows, gs] = (yz * lax.rsqrt(ms + RMS_EPS) * ng_ref[:, gs]).astype(BF16)

    for ci in range(nchunk):
        for g in range(N_GROUPS):
            cb_dot(ci, g)
    for h in range(nh):
        head_matrix(0, h)
    for ci in range(nchunk):
        mxu_steps = []
        for g in range(N_GROUPS):
            mxu_steps += [lambda g=g, pr=pr: pair_dot(ci, g * hpg // 2 + pr) for pr in range(hpg // 2)]
            mxu_steps.append(lambda g=g: group_tail(ci, g))
        vpu_steps = [lambda h=h: head_matrix(ci + 1, h) for h in range(nh)] if ci + 1 < nchunk else []
        ratio = -(-len(vpu_steps) // len(mxu_steps))
        for j, step in enumerate(mxu_steps):
            for v in vpu_steps[j * ratio:(j + 1) * ratio]:
                v()
            step()

    yb = jnp.dot(yn_ref[...], wbo_ref[...], preferred_element_type=F32)
    o_ref[...] = (ya_ref[...].astype(F32) + gb_ref[...].astype(F32) * yb).astype(BF16)


def _ssd_fwd(xc, dt_raw, yoffb, zs, ya, gates, dtb, alog, ef, dsk, ng, wbo, *, d_inner, t_rows=512):
    b, s, c = xc.shape
    d = wbo.shape[1]
    nt = s // t_rows
    nchunk = t_rows // CHUNK
    nh = d_inner // HEAD_DIM
    kern = functools.partial(_ssd_fwd_kernel, t_rows=t_rows, d_inner=d_inner)
    tile = lambda w: pl.BlockSpec((None, t_rows, w), lambda bi, t: (bi, t, 0))
    return pl.pallas_call(
        kern,
        grid=(b, nt),
        in_specs=[
            tile(c), tile(LANES), tile(d_inner), tile(d_inner), tile(d),
            pl.BlockSpec((None, t_rows, d), lambda bi, t: (bi, t, 1)),
            _resident(dtb.shape), _resident(alog.shape), _resident(ef.shape), _resident(dsk.shape),
            _resident(ng.shape), _resident(wbo.shape),
        ],
        out_specs=tile(d),
        out_shape=jax.ShapeDtypeStruct((b, s, d), BF16),
        scratch_shapes=[
            pltpu.VMEM((N_GROUPS, D_STATE, d_inner // N_GROUPS), F32),
            pltpu.VMEM((t_rows, d_inner), BF16),
            pltpu.VMEM((nchunk, N_GROUPS, CHUNK, CHUNK), F32),
            pltpu.VMEM((nchunk, CHUNK, nh * CHUNK), BF16),
            pltpu.VMEM((t_rows, d_inner), F32),
            pltpu.VMEM((t_rows, d_inner), F32),
            pltpu.VMEM((t_rows, d_inner), F32),
            pltpu.VMEM((nchunk * SUBLANES, d_inner), F32),
        ],
        compiler_params=_params("arbitrary", "arbitrary"),
        name="ssd_fwd",
    )(xc, dt_raw, yoffb, zs, ya, gates, dtb, alog, ef, dsk, ng, wbo)


def _ffn_kernel(x_ref, m_ref, p_ref, wo_ref, l1g_ref, l1b_ref, wgu_ref, wd_ref, l2g_ref, l2b_ref, wple_ref,
                pg_ref, wpg_ref, o_ref, *, alpha, ffn, fc, n_sub):
    sub = x_ref.shape[0] // n_sub

    def stages(si):
        rows = slice(si * sub, (si + 1) * sub)
        mix = jnp.dot(m_ref[rows, :], wo_ref[...], preferred_element_type=F32)
        yield
        h = _layer_norm(alpha * x_ref[rows, :] + mix, l1g_ref[...], l1b_ref[...])
        hb = h.astype(BF16)
        yield
        acc = None
        for k in range(ffn // fc):
            g = jnp.dot(hb, wgu_ref[:, k * fc:(k + 1) * fc], preferred_element_type=F32)
            u = jnp.dot(hb, wgu_ref[:, ffn + k * fc:ffn + (k + 1) * fc], preferred_element_type=F32)
            act = (g * _sigmoid(g) * u).astype(BF16)
            part = jnp.dot(act, wd_ref[k * fc:(k + 1) * fc, :], preferred_element_type=F32)
            acc = part if acc is None else acc + part
            yield
        h2 = _layer_norm(alpha * h + acc, l2g_ref[...], l2b_ref[...])
        yield
        e = jnp.dot(p_ref[rows, :].astype(BF16), wple_ref[...], preferred_element_type=F32)
        e = e * lax.rsqrt(jnp.mean(e * e, axis=-1, keepdims=True) + RMS_EPS) * pg_ref[...]
        gate = _sigmoid(jnp.dot(h2.astype(BF16), wpg_ref[...], preferred_element_type=F32))
        o_ref[rows, :] = h2 + e * gate
        yield

    live = []
    for si in range(n_sub):
        live.append(stages(si))
        for gen in list(live):
            if next(gen, StopIteration) is StopIteration:
                live.remove(gen)
    while live:
        for gen in list(live):
            if next(gen, StopIteration) is StopIteration:
                live.remove(gen)


def _ffn(x2d, merged2d, p3, layer, wo, l1g, l1b, wgu, wd, l2g, l2b, wple, pg, wpg, *, alpha, tm=512, n_sub=2):
    m, d = x2d.shape
    ffn = wd.shape[1]
    fc = ffn // 2
    assert fc % LANES == 0
    kern = functools.partial(_ffn_kernel, alpha=alpha, ffn=ffn, fc=fc, n_sub=n_sub)
    row = lambda i: (i, 0)
    return pl.pallas_call(
        kern,
        grid=(m // tm,),
        in_specs=[
            pl.BlockSpec((tm, d), row), pl.BlockSpec((tm, d), row),
            pl.BlockSpec((None, tm, p3.shape[2]), lambda i: (layer, i, 0)),
            _resident_layer(wo, layer), _resident(l1g.shape), _resident(l1b.shape), _resident_layer(wgu, layer),
            _resident_layer(wd, layer), _resident(l2g.shape), _resident(l2b.shape), _resident_layer(wple, layer),
            _resident(pg.shape), _resident_layer(wpg, layer),
        ],
        out_specs=pl.BlockSpec((tm, d), row),
        out_shape=jax.ShapeDtypeStruct((m, d), F32),
        compiler_params=_params("arbitrary"),
        name="ffn",
    )(x2d, merged2d, p3, wo, l1g, l1b, wgu, wd, l2g, l2b, wple, pg, wpg)


def _head_expand_matrix(first_slot, n_heads, d_inner):
    slot = lax.broadcasted_iota(jnp.int32, (LANES, d_inner), 0)
    head = lax.broadcasted_iota(jnp.int32, (LANES, d_inner), 1) // HEAD_DIM
    return (slot == head + first_slot).astype(BF16)


def kernel(x, p, w_in, conv_a_w, conv_a_b, ln_a_g, ln_a_b, w_a_out, ssm_conv_w, ssm_conv_b, a_log, dt_bias, d_skip, ssm_norm_g, w_b_out, w_o, ln1_g, ln1_b, w_gate_up, w_down, ln2_g, ln2_b, w_ple, ple_norm_g, w_ple_gate):
    b, s, d_model = x.shape
    depth = w_in.shape[0]
    conv_dim = conv_a_w.shape[-1]
    d_inner = w_b_out.shape[1]
    xbc_dim = ssm_conv_w.shape[-1]
    n_heads = d_inner // HEAD_DIM
    m = b * s
    alpha = (2 * depth) ** 0.25
    n_main = 2 * conv_dim + 2 * d_model + d_inner + xbc_dim
    assert xbc_dim == d_inner + 2 * N_GROUPS * D_STATE and 2 * n_heads <= LANES and 2 * HEAD_DIM == LANES

    ef = _head_expand_matrix(0, n_heads, d_inner)
    eb = _head_expand_matrix(n_heads, n_heads, d_inner)
    row = lambda v: v.reshape(1, -1).astype(F32)
    slots = lambda v: jnp.pad(v.reshape(1, -1).astype(F32), ((0, 0), (0, LANES - 2 * n_heads)))

    x2d = x.reshape(m, d_model)
    w_in_bf16 = w_in.astype(BF16)
    w_o_bf16, w_gate_up_bf16, w_down_bf16 = w_o.astype(BF16), w_gate_up.astype(BF16), w_down.astype(BF16)
    w_ple_bf16, w_ple_gate_bf16 = w_ple.astype(BF16), w_ple_gate.astype(BF16)
    p3 = p.reshape(depth, m, p.shape[-1])
    assert w_in.shape[2] == n_main + 2 * n_heads
    for i in range(depth):
        u, gates, zs, xc, dt_raw = _inproj(x2d, w_in_bf16, i, ssm_conv_w[i], row(ssm_conv_b[i]), s_len=s,
                                           conv_dim=conv_dim, d_model=d_model, d_inner=d_inner, xbc_dim=xbc_dim)
        dtb, alog = slots(dt_bias[i]), slots(a_log[i])
        dt3 = dt_raw.reshape(b, s, LANES)
        xc3 = xc.reshape(b, s, xbc_dim)
        gates3 = gates.reshape(b, s, 2 * d_model)
        ya, yoffb = _branch_a_ssd_bwd(u.reshape(b, s, conv_dim), gates3, xc3, dt3, conv_a_w[i], row(conv_a_b[i]),
                                      row(ln_a_g[i]), row(ln_a_b[i]), w_a_out[i].astype(BF16), dtb, alog, eb,
                                      d_inner=d_inner)
        merged = _ssd_fwd(xc3, dt3, yoffb, zs.reshape(b, s, d_inner), ya, gates3, dtb, alog, ef,
                          row(jnp.repeat(d_skip[i], HEAD_DIM)), row(ssm_norm_g[i]), w_b_out[i].astype(BF16),
                          d_inner=d_inner)
        x2d = _ffn(x2d, merged.reshape(m, d_model), p3, i, w_o_bf16, row(ln1_g[i]), row(ln1_b[i]),
                   w_gate_up_bf16, w_down_bf16, row(ln2_g[i]), row(ln2_b[i]),
                   w_ple_bf16, row(ple_norm_g[i]), w_ple_gate_bf16, alpha=alpha)
    return x2d.reshape(b, s, d_model)
```

```python
import functools

import jax
import jax.numpy as jnp
from jax import lax
from jax.experimental import pallas as pl
from jax.experimental.pallas import tpu as pltpu

F32 = jnp.float32
BF16 = jnp.bfloat16

CONV_KERNEL = 31
SSM_CONV = 5
HEAD_DIM = 64
N_GROUPS = 8
D_STATE = 128
CHUNK = 128
LN_EPS = 1e-5
RMS_EPS = 1e-6

V7X_VMEM_LIMIT_BYTES = 56 * 1024 * 1024
SUBLANES = 8
LANES = 128
BF16_ROWS = 16

HI = lax.Precision.HIGHEST


def _sigmoid(v):
    return 1.0 / (1.0 + jnp.exp(-v))


def _softplus(v):
    return jnp.maximum(v, 0.0) + jnp.log1p(jnp.exp(-jnp.abs(v)))


def _layer_norm(v, g, b):
    mu = jnp.mean(v, axis=-1, keepdims=True)
    d = v - mu
    var = jnp.mean(d * d, axis=-1, keepdims=True)
    return d * lax.rsqrt(var + LN_EPS) * g + b


def _resident(shape):
    nd = len(shape)
    return pl.BlockSpec(shape, lambda *_: (0,) * nd, pipeline_mode=pl.Buffered(1))


def _resident_layer(stacked, layer):
    shape = stacked.shape[1:]
    return pl.BlockSpec((None,) + shape, lambda *_: (layer,) + (0,) * len(shape), pipeline_mode=pl.Buffered(1))


def _params(*sem):
    return pltpu.CompilerParams(dimension_semantics=sem, vmem_limit_bytes=V7X_VMEM_LIMIT_BYTES)


def _zero_after(v):
    bits = pltpu.bitcast(v[0:SUBLANES, :], jnp.uint32)
    zero = lax.shift_right_logical(lax.shift_right_logical(bits, jnp.uint32(16)), jnp.uint32(16))
    return pltpu.bitcast(zero, F32)[0:1, :]


def _dwconv_block(win_ref, wcs, w_ref, cs, r0, rb, halo, ktaps, after=None):
    pad = (ktaps - 1) // 2
    lo = (halo - pad) // SUBLANES * SUBLANES
    hi = -(-(halo + pad + rb) // SUBLANES) * SUBLANES
    n = hi - lo
    wv = win_ref[r0 + lo:r0 + lo + n, wcs]
    by_shift = {}
    for k in range(ktaps):
        q, r = divmod(halo - lo + k - pad, SUBLANES)
        by_shift.setdefault(r, []).append((k, q))
    acc = None
    for r, taps in by_shift.items():
        sh = wv if r == 0 else pltpu.roll(wv, n - r, axis=0)
        for k, q in taps:
            tap = w_ref[k:k + 1, cs] if after is None else w_ref[k:k + 1, cs] + after
            term = tap * sh[SUBLANES * q:SUBLANES * q + rb]
            acc = term if acc is None else acc + term
    return acc


def _emit_interleaved(mxu_items, vpu_items, chain):
    pending = list(vpu_items)
    last = None
    for i, item in enumerate(mxu_items):
        item()
        ready = [p for p in pending if p[0] <= i]
        slots_left = len(mxu_items) - i
        quota = -(-len(pending) // slots_left)
        for p in ready[:quota]:
            after = _zero_after(last) if chain and last is not None else None
            last = p[1](after)
            pending.remove(p)
    assert not pending


def _inproj_kernel(xp_ref, xm_ref, xn_ref, w_ref, cw_ref, cb_ref, u_ref, gates_ref, zs_ref, xc_ref, dt_ref,
                   *win_refs, nc, rb, halo, tiles_per_seq, conv_dim, d_model, d_inner, xbc_dim):
    tm = xm_ref.shape[0]
    pos = lax.rem(pl.program_id(0), tiles_per_seq)
    xh = jnp.concatenate([jnp.where(pos > 0, xp_ref[...], 0.0),
                          jnp.where(pos < tiles_per_seq - 1, xn_ref[...], 0.0)], axis=0)
    x = xm_ref[...].astype(BF16)
    xw = jnp.concatenate([x, xh.astype(BF16)], axis=0)

    def mm(c0, n):
        return jnp.dot(x, w_ref[:, c0:c0 + n], preferred_element_type=F32)

    off_gates = 2 * conv_dim
    off_z = off_gates + 2 * d_model
    off_xbc = off_z + d_inner
    off_dt = off_xbc + xbc_dim

    def xbc_chunk(k):
        def run():
            r = jnp.dot(xw, w_ref[:, off_xbc + k * nc:off_xbc + (k + 1) * nc], preferred_element_type=F32)
            win = win_refs[k]
            win[0:halo, :] = r[tm:tm + halo]
            win[halo:halo + tm, :] = r[0:tm]
            win[halo + tm:, :] = r[tm + halo:]
        return run

    def conv_block(k, jl, r0):
        def run(after):
            wcs = slice(jl * LANES, (jl + 1) * LANES)
            cs = slice(k * nc + jl * LANES, k * nc + (jl + 1) * LANES)
            v = _dwconv_block(win_refs[k], wcs, cw_ref, cs, r0, rb, halo, SSM_CONV, after=after) + cb_ref[:, cs]
            xc_ref[r0:r0 + rb, cs] = (v * _sigmoid(v)).astype(BF16)
            return v
        return run

    def glu_chunk(k):
        def run():
            a = mm(k * nc, nc)
            g = mm(conv_dim + k * nc, nc)
            u_ref[:, k * nc:(k + 1) * nc] = (a * _sigmoid(g)).astype(BF16)
        return run

    def gate_chunk(k):
        def run():
            gates_ref[:, k * nc:(k + 1) * nc] = _sigmoid(mm(off_gates + k * nc, nc)).astype(BF16)
        return run

    def z_chunk(k):
        def run():
            z = mm(off_z + k * nc, nc)
            zs_ref[:, k * nc:(k + 1) * nc] = (z * _sigmoid(z)).astype(BF16)
        return run

    def dt_chunk():
        n_dt = w_ref.shape[1] - off_dt
        dt_ref[:, 0:n_dt] = mm(off_dt, n_dt)
        dt_ref[:, n_dt:] = jnp.zeros((tm, LANES - n_dt), F32)

    mxu_items, vpu_items = [], []
    for k in range(xbc_dim // nc):
        mxu_items.append(xbc_chunk(k))
        for jl in range(nc // LANES):
            for r0 in range(0, tm, rb):
                vpu_items.append((k, conv_block(k, jl, r0)))
    mxu_items += [glu_chunk(k) for k in range(conv_dim // nc)]
    mxu_items += [gate_chunk(k) for k in range(2 * d_model // nc)]
    mxu_items += [z_chunk(k) for k in range(d_inner // nc)]
    mxu_items.append(dt_chunk)
    _emit_interleaved(mxu_items, vpu_items, chain=True)


def _inproj(x2d, w_all, layer, cw, cb, *, s_len, conv_dim, d_model, d_inner, xbc_dim, tm=256, nc=512, rb=32):
    m, d = x2d.shape
    n_all = w_all.shape[2]
    halo = SUBLANES
    hb = tm // halo
    last = m // halo - 1
    kern = functools.partial(_inproj_kernel, nc=nc, rb=rb, halo=halo, tiles_per_seq=s_len // tm, conv_dim=conv_dim,
                             d_model=d_model, d_inner=d_inner, xbc_dim=xbc_dim)
    row = lambda i: (i, 0)
    return pl.pallas_call(
        kern,
        grid=(m // tm,),
        in_specs=[
            pl.BlockSpec((halo, d), lambda i: (jnp.maximum(i * hb - 1, 0), 0)),
            pl.BlockSpec((tm, d), row),
            pl.BlockSpec((halo, d), lambda i: (jnp.minimum((i + 1) * hb, last), 0)),
            pl.BlockSpec((None, d, n_all), lambda i: (layer, 0, 0), pipeline_mode=pl.Buffered(1)),
            _resident(cw.shape), _resident(cb.shape),
        ],
        out_specs=[
            pl.BlockSpec((tm, conv_dim), row),
            pl.BlockSpec((tm, 2 * d_model), row),
            pl.BlockSpec((tm, d_inner), row),
            pl.BlockSpec((tm, xbc_dim), row),
            pl.BlockSpec((tm, LANES), row),
        ],
        out_shape=[
            jax.ShapeDtypeStruct((m, conv_dim), BF16),
            jax.ShapeDtypeStruct((m, 2 * d_model), BF16),
            jax.ShapeDtypeStruct((m, d_inner), BF16),
            jax.ShapeDtypeStruct((m, xbc_dim), BF16),
            jax.ShapeDtypeStruct((m, LANES), F32),
        ],
        scratch_shapes=[pltpu.VMEM((tm + 2 * halo, nc), F32) for _ in range(xbc_dim // nc)],
        compiler_params=_params("arbitrary"),
        name="inproj",
    )(x2d, x2d, x2d, w_all, cw, cb)


def _expand_heads(v, e_ref):
    return jnp.dot(v.astype(BF16), e_ref[...], preferred_element_type=F32)


def _tri_masks():
    li = lax.broadcasted_iota(jnp.int32, (CHUNK, CHUNK), 0)
    si = lax.broadcasted_iota(jnp.int32, (CHUNK, CHUNK), 1)
    return li, si


def _dt_and_a(dt_raw, dtb_ref, alog_ref):
    dt = _softplus(dt_raw + dtb_ref[...])
    a = dt * (-jnp.exp(alog_ref[...]))
    return dt, a


def _group_slices(xc_ref, rows, g, d_inner):
    gw = d_inner // N_GROUPS
    x_g = xc_ref[rows, g * gw:(g + 1) * gw]
    b_g = xc_ref[rows, d_inner + g * D_STATE:d_inner + (g + 1) * D_STATE]
    c_g = xc_ref[rows, d_inner + (N_GROUPS + g) * D_STATE:d_inner + (N_GROUPS + g + 1) * D_STATE]
    return x_g, b_g, c_g


def _branch_a_ssd_bwd_kernel(up_ref, um_ref, un_ref, ga_ref, xc_ref, dt_ref, cw_ref, cb_ref, lg_ref, lb_ref, wo_ref,
                             dtb_ref, alog_ref, eb_ref, ya_ref, yoff_ref, win_ref, act_ref, state_ref, sb_ref, ob_ref,
                             dec_ref, *, t_rows, nt, rb, halo, d_inner):
    tt = pl.program_id(1)
    t = nt - 1 - tt

    @pl.when(tt == 0)
    def _():
        state_ref[...] = jnp.zeros_like(state_ref)

    c = um_ref.shape[-1]
    win_ref[0:halo, :] = jnp.where(t > 0, up_ref[...].astype(F32), 0.0)
    win_ref[halo:halo + t_rows, :] = um_ref[...].astype(F32)
    win_ref[halo + t_rows:, :] = jnp.where(t < nt - 1, un_ref[...].astype(F32), 0.0)
    def conv_block(j, r0):
        def run(after):
            cs = slice(j * LANES, (j + 1) * LANES)
            v = _dwconv_block(win_ref, cs, cw_ref, cs, r0, rb, halo, CONV_KERNEL, after=after) + cb_ref[:, cs]
            act_ref[r0:r0 + rb, cs] = v
            return v
        return run

    li, si = _tri_masks()
    triu = (si >= li).astype(F32)
    gw = d_inner // N_GROUPS
    nchunk = t_rows // CHUNK

    def decay_forms(ci):
        def run():
            rows = slice(ci * CHUNK, (ci + 1) * CHUNK)
            dt, a = _dt_and_a(dt_ref[rows, :], dtb_ref, alog_ref)
            rcs = jnp.dot(triu, a, precision=HI, preferred_element_type=F32)
            tot = rcs[0:1, :]
            sb_ref[rows, :] = _expand_heads(dt * jnp.exp(tot - rcs), eb_ref)
            ob_ref[rows, :] = _expand_heads(jnp.exp(rcs), eb_ref)
            dec_ref[ci * SUBLANES:(ci + 1) * SUBLANES, :] = _expand_heads(
                jnp.broadcast_to(jnp.exp(tot), (SUBLANES, LANES)), eb_ref)
        return run

    def group_body(ci, g):
        def run():
            rows = slice(ci * CHUNK, (ci + 1) * CHUNK)
            gs = slice(g * gw, (g + 1) * gw)
            x_g, b_g, c_g = _group_slices(xc_ref, rows, g, d_inner)
            st = state_ref[g]
            yoff_ref[rows, gs] = jnp.dot(c_g, st.astype(BF16), preferred_element_type=F32) * ob_ref[rows, gs]
            xdec = (x_g.astype(F32) * sb_ref[rows, gs]).astype(BF16)
            upd = lax.dot_general(b_g, xdec, (((0,), (0,)), ((), ())), preferred_element_type=F32)
            state_ref[g] = st * dec_ref[ci * SUBLANES:ci * SUBLANES + 1, gs] + upd
        return run

    mxu_items = [decay_forms(ci) for ci in range(nchunk)]
    mxu_items += [group_body(ci, g) for ci in reversed(range(nchunk)) for g in range(N_GROUPS)]
    vpu_items = [(0, conv_block(j, r0)) for j in range(c // LANES) for r0 in range(0, t_rows, rb)]
    _emit_interleaved(mxu_items, vpu_items, chain=False)

    v = _layer_norm(act_ref[...], lg_ref[...], lb_ref[...])
    s = (v * _sigmoid(v)).astype(BF16)
    ya = jnp.dot(s, wo_ref[...], preferred_element_type=F32)
    ya_ref[...] = (ya * ga_ref[...].astype(F32)).astype(BF16)


def _branch_a_ssd_bwd(u, gates, xc, dt_raw, cw, cb, lg, lb, wo, dtb, alog, eb, *, d_inner, t_rows=512, rb=32):
    b, s, c = u.shape
    d = wo.shape[1]
    xw = xc.shape[-1]
    nt = s // t_rows
    halo = BF16_ROWS
    hb = t_rows // halo
    last = s // halo - 1
    rev = lambda t: nt - 1 - t
    tile = lambda w: pl.BlockSpec((None, t_rows, w), lambda bi, t: (bi, rev(t), 0))
    kern = functools.partial(_branch_a_ssd_bwd_kernel, t_rows=t_rows, nt=nt, rb=rb, halo=halo, d_inner=d_inner)
    return pl.pallas_call(
        kern,
        grid=(b, nt),
        in_specs=[
            pl.BlockSpec((None, halo, c), lambda bi, t: (bi, jnp.maximum(rev(t) * hb - 1, 0), 0)),
            tile(c),
            pl.BlockSpec((None, halo, c), lambda bi, t: (bi, jnp.minimum((rev(t) + 1) * hb, last), 0)),
            tile(d),
            tile(xw), tile(LANES),
            _resident(cw.shape), _resident(cb.shape), _resident(lg.shape), _resident(lb.shape), _resident(wo.shape),
            _resident(dtb.shape), _resident(alog.shape), _resident(eb.shape),
        ],
        out_specs=[tile(d), tile(d_inner)],
        out_shape=[jax.ShapeDtypeStruct((b, s, d), BF16), jax.ShapeDtypeStruct((b, s, d_inner), F32)],
        scratch_shapes=[
            pltpu.VMEM((t_rows + 2 * halo, c), F32),
            pltpu.VMEM((t_rows, c), F32),
            pltpu.VMEM((N_GROUPS, D_STATE, d_inner // N_GROUPS), F32),
            pltpu.VMEM((t_rows, d_inner), F32),
            pltpu.VMEM((t_rows, d_inner), F32),
            pltpu.VMEM((t_rows // CHUNK * SUBLANES, d_inner), F32),
        ],
        compiler_params=_params("arbitrary", "arbitrary"),
        name="branch_a_ssd_bwd",
    )(u, u, u, gates, xc, dt_raw, cw, cb, lg, lb, wo, dtb, alog, eb)


def _ssd_fwd_kernel(xc_ref, dt_ref, yoffb_ref, zs_ref, ya_ref, gb_ref, dtb_ref, alog_ref, ef_ref, dsk_ref, ng_ref,
                    wbo_ref, o_ref, state_ref, yn_ref, cb_ref, w_ref, y_ref, sf_ref, of_ref, dec_ref,
                    *, t_rows, d_inner):
    t = pl.program_id(1)

    @pl.when(t == 0)
    def _():
        state_ref[...] = jnp.zeros_like(state_ref)

    li, si = _tri_masks()
    tril = (si <= li).astype(F32)
    lower = li > si
    upper = li < si
    lane_lo = si < HEAD_DIM
    gw = d_inner // N_GROUPS
    hpg = gw // HEAD_DIM
    nh = d_inner // HEAD_DIM
    nchunk = t_rows // CHUNK
    chunk_rows = [slice(ci * CHUNK, (ci + 1) * CHUNK) for ci in range(nchunk)]

    forms = []
    for ci, rows in enumerate(chunk_rows):
        dt, a = _dt_and_a(dt_ref[rows, :], dtb_ref, alog_ref)
        acs = jnp.dot(tril, a, precision=HI, preferred_element_type=F32)
        totf = acs[CHUNK - 1:CHUNK, :]
        rcs = totf - acs + a
        sf_ref[rows, :] = _expand_heads(dt * jnp.exp(totf - acs), ef_ref)
        of_ref[rows, :] = _expand_heads(jnp.exp(acs), ef_ref)
        dec_ref[ci * SUBLANES:(ci + 1) * SUBLANES, :] = _expand_heads(
            jnp.broadcast_to(jnp.exp(totf), (SUBLANES, LANES)), ef_ref)
        logdt = jnp.log(dt)
        dtT = dt.T
        forms.append((acs, rcs, (acs - logdt).T, (rcs - logdt).T, jnp.log(dtT[0:nh, :] + dtT[nh:2 * nh, :])))

    def cb_dot(ci, g):
        _, b_g, c_g = _group_slices(xc_ref, chunk_rows[ci], g, d_inner)
        cb_ref[ci, g] = lax.dot_general(c_g, b_g, (((1,), (1,)), ((), ())), preferred_element_type=F32)

    def head_matrix(ci, h):
        acs, rcs, rowf, rowb, drow = forms[ci]
        argf = acs[:, h:h + 1] - rowf[h:h + 1, :]
        argb = rcs[:, nh + h:nh + h + 1] - rowb[nh + h:nh + h + 1, :]
        arg = jnp.where(lower, argf, jnp.where(upper, argb, drow[h:h + 1, :]))
        w_ref[ci, :, h * CHUNK:(h + 1) * CHUNK] = (cb_ref[ci, h // hpg] * jnp.exp(arg)).astype(BF16)

    def pair_dot(ci, pr):
        rows = chunk_rows[ci]
        xp = xc_ref[rows, pr * LANES:(pr + 1) * LANES]
        zero = jnp.zeros_like(xp)
        xbd = jnp.concatenate([jnp.where(lane_lo, xp, zero), jnp.where(lane_lo, zero, xp)], axis=0)
        y_ref[rows, pr * LANES:(pr + 1) * LANES] = jnp.dot(
            w_ref[ci, :, 2 * pr * CHUNK:2 * (pr + 1) * CHUNK], xbd, preferred_element_type=F32)

    def group_tail(ci, g):
        rows = chunk_rows[ci]
        gs = slice(g * gw, (g + 1) * gw)
        x_g, b_g, c_g = _group_slices(xc_ref, rows, g, d_inner)
        st = state_ref[g]
        xf = x_g.astype(F32)
        yo = jnp.dot(c_g, st.astype(BF16), preferred_element_type=F32) * of_ref[rows, gs]
        y = y_ref[rows, gs] + yo + yoffb_ref[rows, gs] + xf * dsk_ref[:, gs]
        xdec = (xf * sf_ref[rows, gs]).astype(BF16)
        upd = lax.dot_general(b_g, xdec, (((0,), (0,)), ((), ())), preferred_element_type=F32)
        state_ref[g] = st * dec_ref[ci * SUBLANES:ci * SUBLANES + 1, gs] + upd
        yz = y * zs_ref[rows, gs].astype(F32)
        ms = jnp.mean(yz * yz, axis=-1, keepdims=True)
        yn_ref[rows, gs] = (yz * lax.rsqrt(ms + RMS_EPS) * ng_ref[:, gs]).astype(BF16)

    for ci in range(nchunk):
        for g in range(N_GROUPS):
            cb_dot(ci, g)
    for h in range(nh):
        head_matrix(0, h)
    for ci in range(nchunk):
        mxu_steps = []
        for g in range(N_GROUPS):
            mxu_steps += [lambda g=g, pr=pr: pair_dot(ci, g * hpg // 2 + pr) for pr in range(hpg // 2)]
            mxu_steps.append(lambda g=g: group_tail(ci, g))
        vpu_steps = [lambda h=h: head_matrix(ci + 1, h) for h in range(nh)] if ci + 1 < nchunk else []
        ratio = -(-len(vpu_steps) // len(mxu_steps))
        for j, step in enumerate(mxu_steps):
            for v in vpu_steps[j * ratio:(j + 1) * ratio]:
                v()
            step()

    yb = jnp.dot(yn_ref[...], wbo_ref[...], preferred_element_type=F32)
    o_ref[...] = (ya_ref[...].astype(F32) + gb_ref[...].astype(F32) * yb).astype(BF16)


def _ssd_fwd(xc, dt_raw, yoffb, zs, ya, gates, dtb, alog, ef, dsk, ng, wbo, *, d_inner, t_rows=512):
    b, s, c = xc.shape
    d = wbo.shape[1]
    nt = s // t_rows
    nchunk = t_rows // CHUNK
    nh = d_inner // HEAD_DIM
    kern = functools.partial(_ssd_fwd_kernel, t_rows=t_rows, d_inner=d_inner)
    tile = lambda w: pl.BlockSpec((None, t_rows, w), lambda bi, t: (bi, t, 0))
    return pl.pallas_call(
        kern,
        grid=(b, nt),
        in_specs=[
            tile(c), tile(LANES), tile(d_inner), tile(d_inner), tile(d),
            pl.BlockSpec((None, t_rows, d), lambda bi, t: (bi, t, 1)),
            _resident(dtb.shape), _resident(alog.shape), _resident(ef.shape), _resident(dsk.shape),
            _resident(ng.shape), _resident(wbo.shape),
        ],
        out_specs=tile(d),
        out_shape=jax.ShapeDtypeStruct((b, s, d), BF16),
        scratch_shapes=[
            pltpu.VMEM((N_GROUPS, D_STATE, d_inner // N_GROUPS), F32),
            pltpu.VMEM((t_rows, d_inner), BF16),
            pltpu.VMEM((nchunk, N_GROUPS, CHUNK, CHUNK), F32),
            pltpu.VMEM((nchunk, CHUNK, nh * CHUNK), BF16),
            pltpu.VMEM((t_rows, d_inner), F32),
            pltpu.VMEM((t_rows, d_inner), F32),
            pltpu.VMEM((t_rows, d_inner), F32),
            pltpu.VMEM((nchunk * SUBLANES, d_inner), F32),
        ],
        compiler_params=_params("arbitrary", "arbitrary"),
        name="ssd_fwd",
    )(xc, dt_raw, yoffb, zs, ya, gates, dtb, alog, ef, dsk, ng, wbo)


def _ffn_kernel(x_ref, m_ref, p_ref, wo_ref, l1g_ref, l1b_ref, wgu_ref, wd_ref, l2g_ref, l2b_ref, wple_ref,
                pg_ref, wpg_ref, o_ref, *, alpha, ffn, fc, n_sub):
    sub = x_ref.shape[0] // n_sub

    def stages(si):
        rows = slice(si * sub, (si + 1) * sub)
        mix = jnp.dot(m_ref[rows, :], wo_ref[...], preferred_element_type=F32)
        yield
        h = _layer_norm(alpha * x_ref[rows, :] + mix, l1g_ref[...], l1b_ref[...])
        hb = h.astype(BF16)
        yield
        acc = None
        for k in range(ffn // fc):
            g = jnp.dot(hb, wgu_ref[:, k * fc:(k + 1) * fc], preferred_element_type=F32)
            u = jnp.dot(hb, wgu_ref[:, ffn + k * fc:ffn + (k + 1) * fc], preferred_element_type=F32)
            act = (g * _sigmoid(g) * u).astype(BF16)
            part = jnp.dot(act, wd_ref[k * fc:(k + 1) * fc, :], preferred_element_type=F32)
            acc = part if acc is None else acc + part
            yield
        h2 = _layer_norm(alpha * h + acc, l2g_ref[...], l2b_ref[...])
        yield
        e = jnp.dot(p_ref[rows, :].astype(BF16), wple_ref[...], preferred_element_type=F32)
        e = e * lax.rsqrt(jnp.mean(e * e, axis=-1, keepdims=True) + RMS_EPS) * pg_ref[...]
        gate = _sigmoid(jnp.dot(h2.astype(BF16), wpg_ref[...], preferred_element_type=F32))
        o_ref[rows, :] = h2 + e * gate
        yield

    live = []
    for si in range(n_sub):
        live.append(stages(si))
        for gen in list(live):
            if next(gen, StopIteration) is StopIteration:
                live.remove(gen)
    while live:
        for gen in list(live):
            if next(gen, StopIteration) is StopIteration:
                live.remove(gen)


def _ffn(x2d, merged2d, p3, layer, wo, l1g, l1b, wgu, wd, l2g, l2b, wple, pg, wpg, *, alpha, tm=512, n_sub=2):
    m, d = x2d.shape
    ffn = wd.shape[1]
    fc = ffn // 2
    assert fc % LANES == 0
    kern = functools.partial(_ffn_kernel, alpha=alpha, ffn=ffn, fc=fc, n_sub=n_sub)
    row = lambda i: (i, 0)
    return pl.pallas_call(
        kern,
        grid=(m // tm,),
        in_specs=[
            pl.BlockSpec((tm, d), row), pl.BlockSpec((tm, d), row),
            pl.BlockSpec((None, tm, p3.shape[2]), lambda i: (layer, i, 0)),
            _resident_layer(wo, layer), _resident(l1g.shape), _resident(l1b.shape), _resident_layer(wgu, layer),
            _resident_layer(wd, layer), _resident(l2g.shape), _resident(l2b.shape), _resident_layer(wple, layer),
            _resident(pg.shape), _resident_layer(wpg, layer),
        ],
        out_specs=pl.BlockSpec((tm, d), row),
        out_shape=jax.ShapeDtypeStruct((m, d), F32),
        compiler_params=_params("arbitrary"),
        name="ffn",
    )(x2d, merged2d, p3, wo, l1g, l1b, wgu, wd, l2g, l2b, wple, pg, wpg)


def _head_expand_matrix(first_slot, n_heads, d_inner):
    slot = lax.broadcasted_iota(jnp.int32, (LANES, d_inner), 0)
    head = lax.broadcasted_iota(jnp.int32, (LANES, d_inner), 1) // HEAD_DIM
    return (slot == head + first_slot).astype(BF16)


def kernel(x, p, w_in, conv_a_w, conv_a_b, ln_a_g, ln_a_b, w_a_out, ssm_conv_w, ssm_conv_b, a_log, dt_bias, d_skip, ssm_norm_g, w_b_out, w_o, ln1_g, ln1_b, w_gate_up, w_down, ln2_g, ln2_b, w_ple, ple_norm_g, w_ple_gate):
    b, s, d_model = x.shape
    depth = w_in.shape[0]
    conv_dim = conv_a_w.shape[-1]
    d_inner = w_b_out.shape[1]
    xbc_dim = ssm_conv_w.shape[-1]
    n_heads = d_inner // HEAD_DIM
    m = b * s
    alpha = (2 * depth) ** 0.25
    n_main = 2 * conv_dim + 2 * d_model + d_inner + xbc_dim
    assert xbc_dim == d_inner + 2 * N_GROUPS * D_STATE and 2 * n_heads <= LANES and 2 * HEAD_DIM == LANES

    ef = _head_expand_matrix(0, n_heads, d_inner)
    eb = _head_expand_matrix(n_heads, n_heads, d_inner)
    row = lambda v: v.reshape(1, -1).astype(F32)
    slots = lambda v: jnp.pad(v.reshape(1, -1).astype(F32), ((0, 0), (0, LANES - 2 * n_heads)))

    x2d = x.reshape(m, d_model)
    w_in_bf16 = w_in.astype(BF16)
    w_o_bf16, w_gate_up_bf16, w_down_bf16 = w_o.astype(BF16), w_gate_up.astype(BF16), w_down.astype(BF16)
    w_ple_bf16, w_ple_gate_bf16 = w_ple.astype(BF16), w_ple_gate.astype(BF16)
    p3 = p.reshape(depth, m, p.shape[-1])
    assert w_in.shape[2] == n_main + 2 * n_heads
    for i in range(depth):
        u, gates, zs, xc, dt_raw = _inproj(x2d, w_in_bf16, i, ssm_conv_w[i], row(ssm_conv_b[i]), s_len=s,
                                           conv_dim=conv_dim, d_model=d_model, d_inner=d_inner, xbc_dim=xbc_dim)
        dtb, alog = slots(dt_bias[i]), slots(a_log[i])
        dt3 = dt_raw.reshape(b, s, LANES)
        xc3 = xc.reshape(b, s, xbc_dim)
        gates3 = gates.reshape(b, s, 2 * d_model)
        ya, yoffb = _branch_a_ssd_bwd(u.reshape(b, s, conv_dim), gates3, xc3, dt3, conv_a_w[i], row(conv_a_b[i]),
                                      row(ln_a_g[i]), row(ln_a_b[i]), w_a_out[i].astype(BF16), dtb, alog, eb,
                                      d_inner=d_inner)
        merged = _ssd_fwd(xc3, dt3, yoffb, zs.reshape(b, s, d_inner), ya, gates3, dtb, alog, ef,
                          row(jnp.repeat(d_skip[i], HEAD_DIM)), row(ssm_norm_g[i]), w_b_out[i].astype(BF16),
                          d_inner=d_inner)
        x2d = _ffn(x2d, merged.reshape(m, d_model), p3, i, w_o_bf16, row(ln1_g[i]), row(ln1_b[i]),
                   w_gate_up_bf16, w_down_bf16, row(ln2_g[i]), row(ln2_b[i]),
                   w_ple_bf16, row(ple_norm_g[i]), w_ple_gate_bf16, alpha=alpha)
    return x2d.reshape(b, s, d_model)
```
